```python
import math
import jax
import jax.numpy as jnp
from jax import lax
import numpy as np

D_MODEL = 1024
BATCH = 8
SEQ = 2048
DEPTH = 4

GRID_W = 64
CTX_LEN = 256
N_MIXERS = 4
GROUP_W = D_MODEL // N_MIXERS
A_HEADS = 4
A_VDIM = GROUP_W // A_HEADS
A_QKDIM = A_VDIM // 2
B_HEADS = 4
B_DIM = GROUP_W // B_HEADS
C_KSIZE = 31
POOL_WINDOWS = (2, 4, 8, 16)
POOL_GROUP = GROUP_W // len(POOL_WINDOWS)
D_FF = 2816
FFN_KSIZE = 3
RET_CHUNK = 128
Q_BLOCK = 128
ROPE_BASE = 10000.0
EPS = 1e-6
N_MOD = 6
OFF_AQ = 0
OFF_BQ = OFF_AQ + GROUP_W
OFF_BG = OFF_BQ + GROUP_W
OFF_C = OFF_BG + GROUP_W
OFF_D = OFF_C + 2 * GROUP_W
OFF_KV = OFF_D + GROUP_W
KV_AK = 0
KV_AV = GROUP_W
KV_BK = 2 * GROUP_W
KV_BV = 3 * GROUP_W
KV_W = 4 * GROUP_W
D_IN = OFF_KV + KV_W

kernel_name = 'hybrid_diffusion_parallel_mixer_block'


def rms_norm(x, g):
    xf = x.astype(jnp.float32)
    y = xf * lax.rsqrt(jnp.mean(xf * xf, axis=-1, keepdims=True) + EPS)
    return (y * g.astype(jnp.float32)).astype(x.dtype)


def head_rms(x, g=None):
    xf = x.astype(jnp.float32)
    y = xf * lax.rsqrt(jnp.mean(xf * xf, axis=-1, keepdims=True) + EPS)
    return y if g is None else y * g.astype(jnp.float32)


def layer_norm(x, g, b):
    xf = x.astype(jnp.float32)
    mu = jnp.mean(xf, axis=-1, keepdims=True)
    var = jnp.mean(jnp.square(xf - mu), axis=-1, keepdims=True)
    y = (xf - mu) * lax.rsqrt(var + EPS)
    return (y * g.astype(jnp.float32) + b.astype(jnp.float32)).astype(x.dtype)


def modulate(h, shift, scale):
    return h * (1 + scale) + shift


def rope_2d(x, rows, cols):
    n, d = x.shape[1], x.shape[-1]
    q = d // 4
    inv = ROPE_BASE ** (-jnp.arange(q, dtype=jnp.float32) / q)
    ang = jnp.stack([rows[:, None] * inv, cols[:, None] * inv], axis=1)
    bshape = (n,) + (1,) * (x.ndim - 3) + (2, q)
    cos = jnp.cos(ang).reshape(bshape).astype(x.dtype)
    sin = jnp.sin(ang).reshape(bshape).astype(x.dtype)
    xr = x.reshape(x.shape[:-1] + (2, 2, q))
    a, b = xr[..., 0, :], xr[..., 1, :]
    return jnp.stack([a * cos - b * sin, a * sin + b * cos], axis=-2).reshape(x.shape)


def diff_attention(q, k, v, lam):
    bsz, nq = q.shape[0], q.shape[1]
    scale = A_QKDIM ** -0.5

    def attend(qb):
        s = jnp.einsum('bqhmd,bkhmd->bhmqk', qb, k).astype(jnp.float32) * scale
        p = jax.nn.softmax(s, axis=-1)
        w = p[:, :, 0] - lam * p[:, :, 1]
        return jnp.einsum('bhqk,bkhd->bqhd', w.astype(v.dtype), v)

    nb = nq // Q_BLOCK
    qb = q.reshape((bsz, nb, Q_BLOCK) + q.shape[2:]).swapaxes(0, 1)
    o = lax.map(attend, qb)
    return o.swapaxes(0, 1).reshape(bsz, nq, o.shape[-2], o.shape[-1])


def retention_scan(q, k, v, log_gamma, s0):
    bsz, n, h, dk = q.shape
    dv = v.shape[-1]
    nc = n // RET_CHUNK
    pos = jnp.arange(RET_CHUNK, dtype=jnp.float32)
    lg = log_gamma[:, None]
    rel = pos[:, None] - pos[None, :]
    intra = jnp.where(rel >= 0, jnp.exp(jnp.maximum(rel, 0.0)[None] * lg[:, :, None]), 0.0)
    q_dec = jnp.exp((pos + 1.0)[None] * lg).T[:, :, None]
    k_dec = jnp.exp((RET_CHUNK - 1.0 - pos)[None] * lg).T[:, :, None]
    c_dec = jnp.exp(RET_CHUNK * log_gamma)[:, None, None]
    to_chunks = lambda t: t.reshape(bsz, nc, RET_CHUNK, h, t.shape[-1]).swapaxes(0, 1)

    def step(state, inp):
        qc, kc, vc = inp
        scores = jnp.einsum('bihd,bjhd->bhij', qc, kc) * intra
        out = (jnp.einsum('bhij,bjhe->bihe', scores, vc)
               + jnp.einsum('bihd,bhde->bihe', qc * q_dec, state))
        state = c_dec * state + jnp.einsum('bjhd,bjhe->bhde', kc * k_dec, vc)
        return state, out

    state, out = lax.scan(step, s0, (to_chunks(q), to_chunks(k), to_chunks(v)))
    return out.swapaxes(0, 1).reshape(bsz, n, h, dv), state


def retention_state(k, v, log_gamma):
    n = k.shape[1]
    w = jnp.exp((n - 1.0 - jnp.arange(n, dtype=jnp.float32))[:, None] * log_gamma[None, :])
    return jnp.einsum('bnhd,bnhe->bhde', k * w[None, :, :, None], v)


def dwconv(x, w, b):
    ksz = w.shape[0]
    y = lax.conv_general_dilated(x, w[:, None, :], window_strides=(1,),
                                 padding=[(ksz // 2, ksz // 2)],
                                 dimension_numbers=('NWC', 'WIO', 'NWC'),
                                 feature_group_count=x.shape[-1])
    return y + b


def conformer_conv(u2, w_dw, b_dw, ln_g, ln_b):
    a, g = jnp.split(u2, 2, axis=-1)
    u = a * jax.nn.sigmoid(g)
    y = layer_norm(dwconv(u, w_dw, b_dw), ln_g, ln_b)
    return jax.nn.silu(y)


def multiscale_pool(u, w_lin, scale):
    bsz, n, _ = u.shape
    t = jnp.arange(n)
    ug = u.reshape(bsz, n, len(POOL_WINDOWS), POOL_GROUP)
    cs = jnp.concatenate([jnp.zeros((bsz, 1) + ug.shape[2:], jnp.float32),
                          jnp.cumsum(ug.astype(jnp.float32), axis=1)], axis=1)
    outs = []
    for gi, win in enumerate(POOL_WINDOWS):
        lo = jnp.clip(t - win // 2, 0, n)
        hi = jnp.clip(t + win // 2, 0, n)
        csg = cs[:, :, gi]
        mean = (csg[:, hi] - csg[:, lo]) / (hi - lo).astype(jnp.float32)[None, :, None]
        outs.append(mean - ug[:, :, gi].astype(jnp.float32))
    pooled = jnp.stack(outs, axis=2).astype(u.dtype)
    y = jnp.einsum('bngc,gcd->bngd', pooled, w_lin).reshape(bsz, n, GROUP_W)
    return y * scale


def conv_ffn(h, w_up, w_dw, b_dw, w_down):
    u = dwconv(h @ w_up, w_dw, b_dw)
    a, g = jnp.split(u, 2, axis=-1)
    return (a * jax.nn.silu(g)) @ w_down


def token_mixers(hx, hc, w_in, w_out, lam_p, subln_g, ret_decay, cdw_w, cdw_b, cln_g, cln_b,
                 pool_w, pool_scale, rows, cols, lambda_init, need_ctx):
    bsz, n, _ = hx.shape
    px = hx @ w_in
    pc = hc @ (w_in if need_ctx else w_in[:, OFF_KV:])
    kvx = px[..., OFF_KV:]
    kvc = pc[..., pc.shape[-1] - KV_W:]
    sl = lambda t, off, w: t[..., off:off + w]
    qk_a = lambda t: t.reshape(t.shape[0], t.shape[1], A_HEADS, 2, A_QKDIM)
    v_a = lambda t: t.reshape(t.shape[0], t.shape[1], A_HEADS, A_VDIM)
    h_b = lambda t: t.reshape(t.shape[0], t.shape[1], B_HEADS, B_DIM)
    flip = lambda t: t[:, ::-1]

    lam = (jnp.exp(jnp.sum(lam_p[0] * lam_p[1]).astype(jnp.float32))
           - jnp.exp(jnp.sum(lam_p[2] * lam_p[3]).astype(jnp.float32)) + lambda_init)
    ak_c = qk_a(sl(kvc, KV_AK, GROUP_W))
    av_c = v_a(sl(kvc, KV_AV, GROUP_W))
    aq_x = rope_2d(qk_a(sl(px, OFF_AQ, GROUP_W)), rows, cols)
    ak_x = rope_2d(qk_a(sl(kvx, KV_AK, GROUP_W)), rows, cols)
    av_x = v_a(sl(kvx, KV_AV, GROUP_W))

    def diff_out(o):
        y = head_rms(o, subln_g) * (1.0 - lambda_init)
        return y.reshape(o.shape[0], o.shape[1], GROUP_W).astype(hx.dtype)

    ya_x = diff_out(diff_attention(aq_x, jnp.concatenate([ak_c, ak_x], axis=1),
                                   jnp.concatenate([av_c, av_x], axis=1), lam))

    lg_f = -jnp.exp(ret_decay[0].astype(jnp.float32))
    lg_b = -jnp.exp(ret_decay[1].astype(jnp.float32))
    bq_x = rope_2d(h_b(sl(px, OFF_BQ, GROUP_W)), rows, cols)
    bk_x = rope_2d(h_b(sl(kvx, KV_BK, GROUP_W)), rows, cols) * (B_DIM ** -0.5)
    bv_x = h_b(sl(kvx, KV_BV, GROUP_W))
    bk_c = h_b(sl(kvc, KV_BK, GROUP_W)) * (B_DIM ** -0.5)
    bv_c = h_b(sl(kvc, KV_BV, GROUP_W))
    if need_ctx:
        bq_c = h_b(sl(pc, OFF_BQ, GROUP_W))
        zeros = jnp.zeros((bsz, B_HEADS, B_DIM, B_DIM), jnp.float32)
        oc_f, s_f = retention_scan(bq_c, bk_c, bv_c, lg_f, zeros)
        oc_b, s_b = retention_scan(flip(bq_c), flip(bk_c), flip(bv_c), lg_b, zeros)
        ob_c = oc_f + flip(oc_b)
    else:
        s_f = retention_state(bk_c, bv_c, lg_f)
        s_b = retention_state(flip(bk_c), flip(bv_c), lg_b)
    ox_f, _ = retention_scan(bq_x, bk_x, bv_x, lg_f, s_f)
    ox_b, _ = retention_scan(flip(bq_x), flip(bk_x), flip(bv_x), lg_b, s_b)

    def ret_out(o, gate):
        y = jax.nn.silu(gate.astype(jnp.float32)) * head_rms(o).reshape(gate.shape)
        return y.astype(hx.dtype)

    yb_x = ret_out(ox_f + flip(ox_b), sl(px, OFF_BG, GROUP_W))

    yc_x = conformer_conv(sl(px, OFF_C, 2 * GROUP_W), cdw_w, cdw_b, cln_g, cln_b).astype(hx.dtype)
    yd_x = multiscale_pool(sl(px, OFF_D, GROUP_W), pool_w, pool_scale).astype(hx.dtype)
    yx = (jnp.concatenate([ya_x, yb_x, yc_x, yd_x], axis=-1) @ w_out).astype(hx.dtype)
    if not need_ctx:
        return yx, None

    ya_c = diff_out(diff_attention(qk_a(sl(pc, OFF_AQ, GROUP_W)), ak_c, av_c, lam))
    yb_c = ret_out(ob_c, sl(pc, OFF_BG, GROUP_W))
    yc_c = conformer_conv(sl(pc, OFF_C, 2 * GROUP_W), cdw_w, cdw_b, cln_g, cln_b).astype(hc.dtype)
    yd_c = multiscale_pool(sl(pc, OFF_D, GROUP_W), pool_w, pool_scale).astype(hc.dtype)
    yc = (jnp.concatenate([ya_c, yb_c, yc_c, yd_c], axis=-1) @ w_out).astype(hc.dtype)
    return yx, yc


def setup_inputs(seed: int = 0) -> dict:
    key = jax.random.key(seed)
    ks = jax.random.split(key, 24)
    nrm = lambda k, shape, s: jax.random.normal(k, shape, jnp.float32) * s
    ret_base = jnp.log(-jnp.log1p(-(2.0 ** (-5.0 - jnp.arange(B_HEADS, dtype=jnp.float32)))))
    return {
        'x': nrm(ks[0], (BATCH, SEQ, D_MODEL), 1.0),
        'c': nrm(ks[1], (BATCH, D_MODEL), 1.0),
        'ctx': nrm(ks[2], (BATCH, CTX_LEN, D_MODEL), 1.0),
        'c_ctx': nrm(ks[3], (D_MODEL,), 1.0),
        'w_mod': nrm(ks[4], (DEPTH, D_MODEL, N_MOD * D_MODEL), 0.5 * D_MODEL ** -0.5),
        'b_mod': nrm(ks[5], (DEPTH, N_MOD * D_MODEL), 0.02),
        'norm1_g': 1.0 + nrm(ks[6], (DEPTH, D_MODEL), 0.02),
        'norm2_g': 1.0 + nrm(ks[7], (DEPTH, D_MODEL), 0.02),
        'w_in': nrm(ks[8], (DEPTH, D_MODEL, D_IN), D_MODEL ** -0.5),
        'w_out': nrm(ks[9], (DEPTH, N_MIXERS * GROUP_W, D_MODEL), (N_MIXERS * GROUP_W) ** -0.5),
        'diff_lambda': nrm(ks[10], (DEPTH, 4, A_QKDIM), 0.1),
        'diff_subln_g': 1.0 + nrm(ks[11], (DEPTH, A_VDIM), 0.02),
        'ret_decay': ret_base[None, None, :] + nrm(ks[12], (DEPTH, 2, B_HEADS), 0.05),
        'conv_dw_w': nrm(ks[13], (DEPTH, C_KSIZE, GROUP_W), C_KSIZE ** -0.5),
        'conv_dw_b': nrm(ks[14], (DEPTH, GROUP_W), 0.02),
        'conv_ln_g': 1.0 + nrm(ks[15], (DEPTH, GROUP_W), 0.02),
        'conv_ln_b': nrm(ks[16], (DEPTH, GROUP_W), 0.02),
        'pool_w': nrm(ks[17], (DEPTH, len(POOL_WINDOWS), POOL_GROUP, POOL_GROUP), POOL_GROUP ** -0.5),
        'pool_scale': 1.0 + nrm(ks[18], (DEPTH, GROUP_W), 0.02),
        'ffn_w_up': nrm(ks[19], (DEPTH, D_MODEL, 2 * D_FF), D_MODEL ** -0.5),
        'ffn_dw_w': nrm(ks[20], (DEPTH, FFN_KSIZE, 2 * D_FF), FFN_KSIZE ** -0.5),
        'ffn_dw_b': nrm(ks[21], (DEPTH, 2 * D_FF), 0.02),
        'ffn_w_down': nrm(ks[22], (DEPTH, D_FF, D_MODEL), D_FF ** -0.5),
        'final_g': 1.0 + nrm(ks[23], (D_MODEL,), 0.02),
    }


def reference(x, c, ctx, c_ctx, w_mod, b_mod, norm1_g, norm2_g, w_in, w_out, diff_lambda,
              diff_subln_g, ret_decay, conv_dw_w, conv_dw_b, conv_ln_g, conv_ln_b, pool_w,
              pool_scale, ffn_w_up, ffn_dw_w, ffn_dw_b, ffn_w_down, final_g):
    n = x.shape[1]
    n_rows = n // GRID_W
    rows = jnp.repeat(jnp.arange(n_rows), GRID_W).astype(jnp.float32)
    cols = jnp.tile(jnp.arange(GRID_W), n_rows).astype(jnp.float32)
    sc = jax.nn.silu(c)
    scc = jax.nn.silu(c_ctx)
    for l in range(DEPTH):
        need_ctx = l < DEPTH - 1
        lambda_init = 0.8 - 0.6 * math.exp(-0.3 * l)
        mx = jnp.split(sc @ w_mod[l] + b_mod[l], N_MOD, axis=-1)
        mx = [m[:, None, :] for m in mx]
        mc = jnp.split(scc @ w_mod[l] + b_mod[l], N_MOD, axis=-1)
        hx = modulate(rms_norm(x, norm1_g[l]), mx[0], mx[1])
        hc = modulate(rms_norm(ctx, norm1_g[l]), mc[0], mc[1])
        yx, yc = token_mixers(hx, hc, w_in[l], w_out[l], diff_lambda[l], diff_subln_g[l],
                              ret_decay[l], conv_dw_w[l], conv_dw_b[l], conv_ln_g[l], conv_ln_b[l],
                              pool_w[l], pool_scale[l], rows, cols, lambda_init, need_ctx)
        x = x + mx[2] * yx
        hx = modulate(rms_norm(x, norm2_g[l]), mx[3], mx[4])
        x = x + mx[5] * conv_ffn(hx, ffn_w_up[l], ffn_dw_w[l], ffn_dw_b[l], ffn_w_down[l])
        if need_ctx:
            ctx = ctx + mc[2] * yc
            hc = modulate(rms_norm(ctx, norm2_g[l]), mc[3], mc[4])
            ctx = ctx + mc[5] * conv_ffn(hc, ffn_w_up[l], ffn_dw_w[l], ffn_dw_b[l], ffn_w_down[l])
    return rms_norm(x, final_g)
```

```python
import functools
import math

import numpy as np
import jax
import jax.numpy as jnp
from jax import lax
from jax.experimental import pallas as pl
from jax.experimental.pallas import tpu as pltpu

F32 = jnp.float32
BF16 = jnp.bfloat16

D_MODEL = 1024
DEPTH = 4
GRID_W = 64
N_MIXERS = 4
GROUP_W = D_MODEL // N_MIXERS
A_HEADS = 4
A_VDIM = GROUP_W // A_HEADS
A_QKDIM = A_VDIM // 2
B_HEADS = 4
B_DIM = GROUP_W // B_HEADS
C_KSIZE = 31
POOL_WINDOWS = (2, 4, 8, 16)
POOL_GROUP = GROUP_W // len(POOL_WINDOWS)
D_FF = 2816
FFN_KSIZE = 3
RET_CHUNK = 128
ROPE_BASE = 10000.0
EPS = 1e-6
N_MOD = 6
OFF_AQ = 0
OFF_BQ = OFF_AQ + GROUP_W
OFF_BG = OFF_BQ + GROUP_W
OFF_C = OFF_BG + GROUP_W
OFF_D = OFF_C + 2 * GROUP_W
OFF_KV = OFF_D + GROUP_W
KV_AK = 0
KV_AV = GROUP_W
KV_BK = 2 * GROUP_W
KV_BV = 3 * GROUP_W
D_IN = OFF_KV + 4 * GROUP_W

LANES = 128
BF16_ROWS = 16
MOD_ROWS = 16
TOKEN_TILE = 512
ATTN_Q_TILE = 256
ATTN_ROW_BLOCK = 32
CONV_ROW_BLOCK = 32
PAD_ROWS = 16
FFN_CHUNK = 256
VMEM_LIMIT = 56 * 1024 * 1024


def _cparams(*sem):
    return pltpu.CompilerParams(dimension_semantics=sem, vmem_limit_bytes=VMEM_LIMIT)


def _sigmoid(x):
    return 1.0 / (1.0 + jnp.exp(-x))


def _silu(x):
    return x * _sigmoid(x)


def _dot(a, b):
    return jnp.dot(a, b, preferred_element_type=F32)


def _dot_nt(a, b):
    return lax.dot_general(a, b, (((1,), (1,)), ((), ())), preferred_element_type=F32)


def _rope_tables(n, blk):
    lane = np.arange(LANES)
    freq = lane % blk
    is_a = (lane % (2 * blk)) < blk
    is_row = (lane % (4 * blk)) < 2 * blk
    inv = ROPE_BASE ** (-freq.astype(np.float64) / blk)
    t = np.arange(n)
    pos = np.where(is_row[None, :], (t // GRID_W)[:, None], (t % GRID_W)[:, None]).astype(np.float64)
    ang = pos * inv[None, :]
    cos = np.cos(ang)
    sin = np.sin(ang) * np.where(is_a, -1.0, 1.0)[None, :]
    return jnp.asarray(cos, F32), jnp.asarray(sin, F32)


def _rope(x, cos, sin, blk):
    lane = lax.broadcasted_iota(jnp.int32, (1, LANES), 1)
    is_a = (lane % (2 * blk)) < blk
    outs = []
    for j in range(GROUP_W // LANES):
        xh = x[:, j * LANES:(j + 1) * LANES]
        from_hi = pltpu.roll(xh, LANES - blk, axis=1)
        from_lo = pltpu.roll(xh, blk, axis=1)
        outs.append(xh * cos + jnp.where(is_a, from_hi, from_lo) * sin)
    return jnp.concatenate(outs, axis=1)


def _head_mean(a, width):
    r = lax.broadcasted_iota(jnp.int32, (GROUP_W, GROUP_W), 0) // width
    c = lax.broadcasted_iota(jnp.int32, (GROUP_W, GROUP_W), 1) // width
    seg = jnp.where(r == c, 1.0 / width, 0.0).astype(BF16)
    a1 = a.astype(BF16)
    r1 = a - a1.astype(F32)
    a2 = r1.astype(BF16)
    a3 = (r1 - a2.astype(F32)).astype(BF16)
    return _dot(a1, seg) + _dot(a2, seg) + _dot(a3, seg)


def _mod_kernel(s_ref, w_ref, b_ref, o_ref):
    s = _silu(s_ref[...])
    o_ref[0] = _dot(s.astype(BF16), w_ref[0].astype(BF16)) + b_ref[0]


def _modulation(cond, w_mod, b_mod):
    depth = w_mod.shape[0]
    return pl.pallas_call(
        _mod_kernel,
        grid=(depth, N_MOD),
        in_specs=[
            pl.BlockSpec((MOD_ROWS, D_MODEL), lambda l, j: (0, 0)),
            pl.BlockSpec((1, D_MODEL, D_MODEL), lambda l, j: (l, 0, j)),
            pl.BlockSpec((1, 1, D_MODEL), lambda l, j: (l, 0, j)),
        ],
        out_specs=pl.BlockSpec((1, MOD_ROWS, D_MODEL), lambda l, j: (l, 0, j)),
        out_shape=jax.ShapeDtypeStruct((depth, MOD_ROWS, N_MOD * D_MODEL), F32),
        compiler_params=_cparams("arbitrary", "arbitrary"),
        name="modulation",
    )(cond, w_mod, b_mod.reshape(depth, 1, N_MOD * D_MODEL))


def _norm_mod(x, g, shift, scale):
    ms = jnp.mean(x * x, axis=-1, keepdims=True)
    return (x * lax.rsqrt(ms + EPS) * g) * (1.0 + scale) + shift


def _in_kernel(x_ref, mod_ref, g_ref, w_ref, ca_ref, sa_ref, cb_ref, sb_ref,
               aq_ref, ak_ref, av_ref, bq_ref, bk_ref, bv_ref, bg_ref, uc_ref, ud_ref):
    h = _norm_mod(x_ref[...], g_ref[...], mod_ref[0, 0:1, :], mod_ref[0, 1:2, :]).astype(BF16)

    def proj(off, width=GROUP_W):
        return _dot(h, w_ref[:, off:off + width])

    ca, sa, cb, sb = ca_ref[...], sa_ref[...], cb_ref[...], sb_ref[...]
    aq_ref[...] = (_rope(proj(OFF_AQ), ca, sa, A_QKDIM // 4) * (A_QKDIM ** -0.5)).astype(BF16)
    ak_ref[...] = _rope(proj(OFF_KV + KV_AK), ca, sa, A_QKDIM // 4).astype(BF16)
    av_ref[...] = proj(OFF_KV + KV_AV).astype(BF16)
    bq_ref[...] = _rope(proj(OFF_BQ), cb, sb, B_DIM // 4)
    bk_ref[...] = _rope(proj(OFF_KV + KV_BK), cb, sb, B_DIM // 4) * (B_DIM ** -0.5)
    bv_ref[...] = proj(OFF_KV + KV_BV).astype(BF16)
    bg_ref[...] = proj(OFF_BG)
    uc_ref[...] = proj(OFF_C, 2 * GROUP_W)
    ud_ref[...] = proj(OFF_D)


def _in_proj(x2, mod, g, w_bf, tables, seq):
    t = x2.shape[0]
    tm = min(TOKEN_TILE, seq)
    per_seq = seq // tm
    tok = lambda i: (i, 0)
    tab = pl.BlockSpec((tm, LANES), lambda i: (i % per_seq, 0))
    gw = lambda dt: jax.ShapeDtypeStruct((t, GROUP_W), dt)
    gspec = pl.BlockSpec((tm, GROUP_W), tok)
    return pl.pallas_call(
        _in_kernel,
        grid=(t // tm,),
        in_specs=[
            pl.BlockSpec((tm, D_MODEL), tok),
            pl.BlockSpec((1, N_MOD, D_MODEL), lambda i: (i // per_seq, 0, 0)),
            pl.BlockSpec((1, D_MODEL), lambda i: (0, 0)),
            pl.BlockSpec((D_MODEL, D_IN), lambda i: (0, 0)),
            tab, tab, tab, tab,
        ],
        out_specs=[gspec] * 7 + [pl.BlockSpec((tm, 2 * GROUP_W), tok), gspec],
        out_shape=[gw(BF16), gw(BF16), gw(BF16), gw(F32), gw(F32), gw(BF16), gw(F32),
                   jax.ShapeDtypeStruct((t, 2 * GROUP_W), F32), gw(F32)],
        compiler_params=_cparams("arbitrary"),
        name="in_proj",
    )(x2, mod, g, w_bf, *tables)


def _attn_kernel(nseg, lambda_init, q_ref, *refs):
    k_refs = refs[:nseg]
    v_refs = refs[nseg:2 * nseg]
    lam_ref, g_ref, o_ref, s_scr, p_scr = refs[2 * nseg:]
    tq = q_ref.shape[1]
    lp = lam_ref[...]
    lam = (jnp.exp(jnp.sum(lp[0:1] * lp[1:2], axis=-1, keepdims=True))
           - jnp.exp(jnp.sum(lp[2:3] * lp[3:4], axis=-1, keepdims=True)) + lambda_init)
    q = q_ref[0]
    lane = lax.broadcasted_iota(jnp.int32, (1, GROUP_W), 1)
    o = jnp.zeros((tq, GROUP_W), F32)
    for hd in range(A_HEADS):
        for m in range(2):
            lo = hd * A_VDIM + m * A_QKDIM
            qm = jnp.where((lane >= lo) & (lane < lo + A_QKDIM), q, jnp.zeros_like(q))
            off = 0
            for k_ref in k_refs:
                nk = k_ref.shape[1]
                s_scr[m, :, off:off + nk] = _dot_nt(qm, k_ref[0])
                off += nk

        def rows(i, carry):
            r = pl.multiple_of(i * ATTN_ROW_BLOCK, ATTN_ROW_BLOCK)
            s0 = s_scr[0, pl.ds(r, ATTN_ROW_BLOCK), :]
            s1 = s_scr[1, pl.ds(r, ATTN_ROW_BLOCK), :]
            e0 = jnp.exp(s0 - jnp.max(s0, axis=-1, keepdims=True))
            e1 = jnp.exp(s1 - jnp.max(s1, axis=-1, keepdims=True))
            w = (e0 * (1.0 / jnp.sum(e0, axis=-1, keepdims=True))
                 - e1 * (lam / jnp.sum(e1, axis=-1, keepdims=True)))
            p_scr[pl.ds(r, ATTN_ROW_BLOCK), :] = w.astype(BF16)
            return carry

        lax.fori_loop(0, tq // ATTN_ROW_BLOCK, rows, 0)
        oh = jnp.zeros((tq, GROUP_W), F32)
        off = 0
        for v_ref in v_refs:
            nk = v_ref.shape[1]
            oh = oh + _dot(p_scr[:, off:off + nk], v_ref[0])
            off += nk
        o = jnp.where((lane >= hd * A_VDIM) & (lane < (hd + 1) * A_VDIM), oh, o)
    y = o * lax.rsqrt(_head_mean(o * o, A_VDIM) + EPS) * g_ref[...] * (1.0 - lambda_init)
    o_ref[0] = y.astype(BF16)


def _attention(q, ks, vs, lam_p, subln_g, lambda_init):
    b, nq, _ = q.shape
    tq = min(ATTN_Q_TILE, nq)
    nseg = len(ks)
    nk_total = sum(k.shape[1] for k in ks)
    full = lambda a: pl.BlockSpec((1, a.shape[1], GROUP_W), lambda bi, i: (bi, 0, 0))
    return pl.pallas_call(
        functools.partial(_attn_kernel, nseg, lambda_init),
        grid=(b, nq // tq),
        in_specs=[pl.BlockSpec((1, tq, GROUP_W), lambda bi, i: (bi, i, 0))]
                 + [full(k) for k in ks] + [full(v) for v in vs]
                 + [pl.BlockSpec((4, A_QKDIM), lambda bi, i: (0, 0)),
                    pl.BlockSpec((1, GROUP_W), lambda bi, i: (0, 0))],
        out_specs=pl.BlockSpec((1, tq, GROUP_W), lambda bi, i: (bi, i, 0)),
        out_shape=jax.ShapeDtypeStruct((b, nq, GROUP_W), BF16),
        scratch_shapes=[pltpu.VMEM((2, tq, nk_total), F32), pltpu.VMEM((tq, nk_total), BF16)],
        compiler_params=_cparams("arbitrary", "arbitrary"),
        name="diff_attention",
    )(q, *ks, *vs, lam_p, jnp.tile(subln_g, A_HEADS).reshape(1, GROUP_W))


def _ret_kernel(q_ref, k_ref, v_ref, gate_ref, dec_ref, s0_ref, y_ref, st_ref, o_scr, s_scr):
    n = q_ref.shape[1]
    c = RET_CHUNK
    nch = n // c
    lane = lax.broadcasted_iota(jnp.int32, (1, GROUP_W), 1)
    head_masks = [(lane >= h * B_DIM) & (lane < (h + 1) * B_DIM) for h in range(B_HEADS)]
    rr = lax.broadcasted_iota(jnp.int32, (GROUP_W, GROUP_W), 0) // B_DIM
    cc = lax.broadcasted_iota(jnp.int32, (GROUP_W, GROUP_W), 1) // B_DIM
    block_diag = rr == cc
    pos = lax.broadcasted_iota(jnp.int32, (c, 1), 0).astype(F32)
    ii = lax.broadcasted_iota(jnp.int32, (c, c), 0)
    jj = lax.broadcasted_iota(jnp.int32, (c, c), 1)

    for d in range(2):
        lg = -jnp.exp(dec_ref[d:d + 1, :])
        rel = (ii - jj) if d == 0 else (jj - ii)
        relf = jnp.maximum(rel, 0).astype(F32)
        intra = jnp.concatenate(
            [jnp.where(rel >= 0, jnp.exp(relf * lg[:, h * B_DIM:h * B_DIM + 1]), 0.0) for h in range(B_HEADS)],
            axis=0)
        if d == 0:
            q_dec = jnp.exp((pos + 1.0) * lg)
            k_dec = jnp.exp((c - 1.0 - pos) * lg)
        else:
            q_dec = jnp.exp((c - pos) * lg)
            k_dec = jnp.exp(pos * lg)
        c_dec = jnp.exp(float(c) * lg)
        s_scr[...] = s0_ref[0, d]

        def chunk(i, carry):
            ci = i if d == 0 else nch - 1 - i
            r = pl.multiple_of(ci * c, c)
            qc = q_ref[0, pl.ds(r, c), :]
            kc = k_ref[0, pl.ds(r, c), :]
            vc = v_ref[0, pl.ds(r, c), :]
            qb = qc.astype(BF16)
            q4 = jnp.concatenate([jnp.where(mk, qb, jnp.zeros_like(qb)) for mk in head_masks], axis=0)
            sc = (_dot_nt(q4, kc.astype(BF16)) * intra).astype(BF16)
            state = s_scr[...]
            out = _dot((qc * q_dec).astype(BF16), state.astype(BF16))
            for h in range(B_HEADS):
                out = out + jnp.where(head_masks[h], _dot(sc[h * c:(h + 1) * c], vc), 0.0)
            if d == 0:
                o_scr[pl.ds(r, c), :] = out
            else:
                o_scr[pl.ds(r, c), :] = o_scr[pl.ds(r, c), :] + out
            kd_t = (kc * k_dec).T.astype(BF16)
            s_scr[...] = c_dec * state + jnp.where(block_diag, _dot(kd_t, vc), 0.0)
            return carry

        lax.fori_loop(0, nch, chunk, 0)
        st_ref[0, d] = s_scr[...]

    def finish(i, carry):
        r = pl.multiple_of(i * c, c)
        o = o_scr[pl.ds(r, c), :]
        y = _silu(gate_ref[0, pl.ds(r, c), :]) * (o * lax.rsqrt(_head_mean(o * o, B_DIM) + EPS))
        y_ref[0, pl.ds(r, c), :] = y.astype(BF16)
        return carry

    lax.fori_loop(0, nch, finish, 0)


def _retention(q, k, v, gate, dec_lanes, s0):
    b, n, _ = q.shape
    seq = pl.BlockSpec((1, n, GROUP_W), lambda bi: (bi, 0, 0))
    st = pl.BlockSpec((1, 2, GROUP_W, GROUP_W), lambda bi: (bi, 0, 0, 0))
    return pl.pallas_call(
        _ret_kernel,
        grid=(b,),
        in_specs=[seq, seq, seq, seq, pl.BlockSpec((2, GROUP_W), lambda bi: (0, 0)), st],
        out_specs=[seq, st],
        out_shape=[jax.ShapeDtypeStruct((b, n, GROUP_W), BF16),
                   jax.ShapeDtypeStruct((b, 2, GROUP_W, GROUP_W), F32)],
        scratch_shapes=[pltpu.VMEM((n, GROUP_W), F32), pltpu.VMEM((GROUP_W, GROUP_W), F32)],
        compiler_params=_cparams("arbitrary"),
        name="retention",
    )(q, k, v, gate, dec_lanes, s0)


def _convpool_kernel(uc_ref, ud_ref, cw_ref, cb_ref, lg_ref, lb_ref, pw_ref, ps_ref, yc_ref, yd_ref,
                     ubuf, dbuf):
    n = uc_ref.shape[1]
    rb = CONV_ROW_BLOCK
    zeros = jnp.zeros((PAD_ROWS, GROUP_W), F32)
    for buf in (ubuf, dbuf):
        buf[0:PAD_ROWS, :] = zeros
        buf[PAD_ROWS + n:PAD_ROWS + n + PAD_ROWS, :] = zeros

    def fill(i, carry):
        r = pl.multiple_of(i * RET_CHUNK, RET_CHUNK)
        u2 = uc_ref[0, pl.ds(r, RET_CHUNK), :]
        ubuf[pl.ds(PAD_ROWS + r, RET_CHUNK), :] = u2[:, :GROUP_W] * _sigmoid(u2[:, GROUP_W:])
        dbuf[pl.ds(PAD_ROWS + r, RET_CHUNK), :] = ud_ref[0, pl.ds(r, RET_CHUNK), :]
        return carry

    lax.fori_loop(0, n // RET_CHUNK, fill, 0)

    lane = lax.broadcasted_iota(jnp.int32, (1, GROUP_W), 1)
    half = jnp.left_shift(1, lane // POOL_GROUP)
    max_half = POOL_WINDOWS[-1] // 2

    def block(i, carry):
        r = pl.multiple_of(i * rb, rb)
        uwin = ubuf[pl.ds(r, rb + 2 * PAD_ROWS), :]
        dwin = dbuf[pl.ds(r, rb + 2 * PAD_ROWS), :]
        acc = jnp.zeros((rb, GROUP_W), F32) + cb_ref[...]
        for kk in range(C_KSIZE):
            s = PAD_ROWS - C_KSIZE // 2 + kk
            acc = acc + uwin[s:s + rb] * cw_ref[kk:kk + 1, :]
        mu = jnp.mean(acc, axis=-1, keepdims=True)
        var = jnp.mean(jnp.square(acc - mu), axis=-1, keepdims=True)
        yn = (acc - mu) * lax.rsqrt(var + EPS) * lg_ref[...] + lb_ref[...]
        yc_ref[0, pl.ds(r, rb), :] = _silu(yn).astype(BF16)
        tot = jnp.zeros((rb, GROUP_W), F32)
        for j in range(-max_half, max_half):
            inside = (half > j) if j >= 0 else (half >= -j)
            tot = tot + jnp.where(inside, dwin[PAD_ROWS + j:PAD_ROWS + j + rb], 0.0)
        t = r + lax.broadcasted_iota(jnp.int32, (rb, 1), 0)
        cnt = (jnp.minimum(t + half, n) - jnp.maximum(t - half, 0)).astype(F32)
        pooled = tot / cnt - dwin[PAD_ROWS:PAD_ROWS + rb]
        yd_ref[0, pl.ds(r, rb), :] = (_dot(pooled.astype(BF16), pw_ref[...]) * ps_ref[...]).astype(BF16)
        return carry

    lax.fori_loop(0, n // rb, block, 0)


def _convpool(uc, ud, cw, cb, lng, lnb, pw_bd, pscale):
    b, n, _ = ud.shape
    seq = lambda w: pl.BlockSpec((1, n, w), lambda bi: (bi, 0, 0))
    row = pl.BlockSpec((1, GROUP_W), lambda bi: (0, 0))
    return pl.pallas_call(
        _convpool_kernel,
        grid=(b,),
        in_specs=[seq(2 * GROUP_W), seq(GROUP_W),
                  pl.BlockSpec((C_KSIZE, GROUP_W), lambda bi: (0, 0)), row, row, row,
                  pl.BlockSpec((GROUP_W, GROUP_W), lambda bi: (0, 0)), row],
        out_specs=[seq(GROUP_W), seq(GROUP_W)],
        out_shape=[jax.ShapeDtypeStruct((b, n, GROUP_W), BF16)] * 2,
        scratch_shapes=[pltpu.VMEM((n + 2 * PAD_ROWS, GROUP_W), F32)] * 2,
        compiler_params=_cparams("arbitrary"),
        name="conv_pool",
    )(uc, ud, cw, cb, lng, lnb, pw_bd, pscale)


def _out_kernel(ya_ref, yb_ref, yc_ref, yd_ref, w_ref, x_ref, mod_ref, g_ref, xo_ref, h_ref):
    y = jnp.zeros(x_ref.shape, F32)
    for gi, y_ref in enumerate((ya_ref, yb_ref, yc_ref, yd_ref)):
        y = y + _dot(y_ref[...], w_ref[gi * GROUP_W:(gi + 1) * GROUP_W, :])
    x = x_ref[...] + mod_ref[0, 2:3, :] * y
    xo_ref[...] = x
    h_ref[...] = _norm_mod(x, g_ref[...], mod_ref[0, 3:4, :], mod_ref[0, 4:5, :]).astype(BF16)


def _out_proj(ys, w_bf, x2, mod, g, seq):
    t = x2.shape[0]
    tm = min(TOKEN_TILE, seq)
    per_seq = seq // tm
    tok = lambda i: (i, 0)
    gspec = pl.BlockSpec((tm, GROUP_W), tok)
    xspec = pl.BlockSpec((tm, D_MODEL), tok)
    return pl.pallas_call(
        _out_kernel,
        grid=(t // tm,),
        in_specs=[gspec] * 4 + [pl.BlockSpec((D_MODEL, D_MODEL), lambda i: (0, 0)), xspec,
                                pl.BlockSpec((1, N_MOD, D_MODEL), lambda i: (i // per_seq, 0, 0)),
                                pl.BlockSpec((1, D_MODEL), lambda i: (0, 0))],
        out_specs=[xspec, xspec],
        out_shape=[jax.ShapeDtypeStruct((t, D_MODEL), F32), jax.ShapeDtypeStruct((t, D_MODEL), BF16)],
        compiler_params=_cparams("arbitrary"),
        name="out_proj",
    )(*ys, w_bf, x2, mod, g)


def _ffn_kernel(per_seq, final, h_ref, hp_ref, hn_ref, wu_ref, dw_ref, db_ref, wd_ref, x_ref, mod_ref, fg_ref,
                o_ref, hbuf, ubuf, acc):
    tm = h_ref.shape[0]
    i = pl.program_id(0)
    first = (i % per_seq) == 0
    last = (i % per_seq) == per_seq - 1
    hp = hp_ref[...]
    hn = hn_ref[...]
    hbuf[0:BF16_ROWS, :] = jnp.where(first, jnp.zeros_like(hp), hp)
    hbuf[BF16_ROWS:BF16_ROWS + tm, :] = h_ref[...]
    hbuf[BF16_ROWS + tm:, :] = jnp.where(last, jnp.zeros_like(hn), hn)
    acc[...] = jnp.zeros_like(acc)
    tf = FFN_CHUNK

    def chunk(j, carry):
        ubuf[...] = _dot(hbuf[...], wu_ref[j])
        w = dw_ref[j]
        u = db_ref[j]
        for kk in range(FFN_KSIZE):
            u = u + ubuf[pl.ds(BF16_ROWS - FFN_KSIZE // 2 + kk, tm), :] * w[kk:kk + 1, :]
        act = (u[:, :tf] * _silu(u[:, tf:])).astype(BF16)
        acc[...] = acc[...] + _dot(act, wd_ref[j])
        return carry

    lax.fori_loop(0, wu_ref.shape[0], chunk, 0)
    x = x_ref[...] + mod_ref[0, 5:6, :] * acc[...]
    if final:
        ms = jnp.mean(x * x, axis=-1, keepdims=True)
        x = x * lax.rsqrt(ms + EPS) * fg_ref[...]
    o_ref[...] = x


def _conv_ffn(h2, wu_r, dw_r, db_r, wd_r, x2, mod, final_g, seq, final):
    t = x2.shape[0]
    tm = min(TOKEN_TILE, seq)
    per_seq = seq // tm
    nblk = tm // BF16_ROWS
    tok = lambda i: (i, 0)
    nj = wu_r.shape[0]
    const3 = lambda i: (0, 0, 0)
    halo = lambda f: pl.BlockSpec((BF16_ROWS, D_MODEL), f)
    xspec = pl.BlockSpec((tm, D_MODEL), tok)
    return pl.pallas_call(
        functools.partial(_ffn_kernel, per_seq, final),
        grid=(t // tm,),
        in_specs=[xspec,
                  halo(lambda i: (jnp.maximum(i * nblk - 1, 0), 0)),
                  halo(lambda i: (jnp.minimum((i + 1) * nblk, t // BF16_ROWS - 1), 0)),
                  pl.BlockSpec((nj, D_MODEL, 2 * FFN_CHUNK), const3),
                  pl.BlockSpec((nj, FFN_KSIZE, 2 * FFN_CHUNK), const3),
                  pl.BlockSpec((nj, 1, 2 * FFN_CHUNK), const3),
                  pl.BlockSpec((nj, FFN_CHUNK, D_MODEL), const3),
                  xspec,
                  pl.BlockSpec((1, N_MOD, D_MODEL), lambda i: (i // per_seq, 0, 0)),
                  pl.BlockSpec((1, D_MODEL), lambda i: (0, 0))],
        out_specs=xspec,
        out_shape=jax.ShapeDtypeStruct((t, D_MODEL), F32),
        scratch_shapes=[pltpu.VMEM((tm + 2 * BF16_ROWS, D_MODEL), BF16),
                        pltpu.VMEM((tm + 2 * BF16_ROWS, 2 * FFN_CHUNK), F32),
                        pltpu.VMEM((tm, D_MODEL), F32)],
        compiler_params=_cparams("arbitrary"),
        name="conv_ffn",
    )(h2, h2, h2, wu_r, dw_r, db_r, wd_r, x2, mod, final_g)


def _split_ffn(a, lead):
    nj = D_FF // FFN_CHUNK
    v = a[..., :D_FF].reshape(lead + (nj, FFN_CHUNK))
    g = a[..., D_FF:].reshape(lead + (nj, FFN_CHUNK))
    both = jnp.concatenate([v, g], axis=-1)
    return jnp.moveaxis(both, -2, 0)


def kernel(x, c, ctx, c_ctx, w_mod, b_mod, norm1_g, norm2_g, w_in, w_out, diff_lambda, diff_subln_g, ret_decay,
           conv_dw_w, conv_dw_b, conv_ln_g, conv_ln_b, pool_w, pool_scale, ffn_w_up, ffn_dw_w, ffn_dw_b,
           ffn_w_down, final_g):
    b, n, d = x.shape
    nc = ctx.shape[1]
    depth = w_mod.shape[0]
    assert d == D_MODEL and b + 1 <= MOD_ROWS
    assert n % RET_CHUNK == 0 and nc % RET_CHUNK == 0 and n % min(TOKEN_TILE, n) == 0

    cond = jnp.zeros((MOD_ROWS, D_MODEL), F32).at[:b].set(c).at[b].set(c_ctx)
    mods = _modulation(cond, w_mod, b_mod)

    tab_x = _rope_tables(n, A_QKDIM // 4) + _rope_tables(n, B_DIM // 4)
    one, zero = jnp.ones((nc, LANES), F32), jnp.zeros((nc, LANES), F32)
    tab_c = (one, zero, one, zero)

    x2 = x.reshape(b * n, d)
    c2 = ctx.reshape(b * nc, d)
    row = lambda v: v.reshape(1, -1)
    g3 = lambda a, s: a.reshape(b, s, GROUP_W)
    zero_state = jnp.zeros((b, 2, GROUP_W, GROUP_W), F32)

    for l in range(depth):
        need_ctx = l < depth - 1
        lambda_init = 0.8 - 0.6 * math.exp(-0.3 * l)
        mod_x = mods[l, :b].reshape(b, N_MOD, D_MODEL)
        mod_c = jnp.broadcast_to(mods[l, b].reshape(1, N_MOD, D_MODEL), (b, N_MOD, D_MODEL))
        w_in_bf = w_in[l].astype(BF16)
        w_out_bf = w_out[l].astype(BF16)
        wu_r = _split_ffn(ffn_w_up[l].astype(BF16), (D_MODEL,))
        dw_r = _split_ffn(ffn_dw_w[l], (FFN_KSIZE,))
        db_r = _split_ffn(ffn_dw_b[l].reshape(1, -1), (1,))
        wd_r = ffn_w_down[l].astype(BF16).reshape(D_FF // FFN_CHUNK, FFN_CHUNK, D_MODEL)
        dec_lanes = jnp.repeat(ret_decay[l], B_DIM, axis=-1)
        pw_bd = jnp.zeros((GROUP_W, GROUP_W), F32)
        for gi in range(len(POOL_WINDOWS)):
            pw_bd = pw_bd.at[gi * POOL_GROUP:(gi + 1) * POOL_GROUP, gi * POOL_GROUP:(gi + 1) * POOL_GROUP].set(
                pool_w[l, gi])
        pw_bd = pw_bd.astype(BF16)
        mixer_params = (conv_dw_w[l], row(conv_dw_b[l]), row(conv_ln_g[l]), row(conv_ln_b[l]), pw_bd,
                        row(pool_scale[l]))

        px = _in_proj(x2, mod_x, row(norm1_g[l]), w_in_bf, tab_x, n)
        pc = _in_proj(c2, mod_c, row(norm1_g[l]), w_in_bf, tab_c, nc)
        aq_x, ak_x, av_x, bq_x, bk_x, bv_x, bg_x, uc_x, ud_x = px
        aq_c, ak_c, av_c, bq_c, bk_c, bv_c, bg_c, uc_c, ud_c = pc

        ya_x = _attention(g3(aq_x, n), [g3(ak_c, nc), g3(ak_x, n)], [g3(av_c, nc), g3(av_x, n)],
                          diff_lambda[l], diff_subln_g[l], lambda_init)
        yb_c, st_c = _retention(g3(bq_c, nc), g3(bk_c, nc), g3(bv_c, nc), g3(bg_c, nc), dec_lanes, zero_state)
        yb_x, _ = _retention(g3(bq_x, n), g3(bk_x, n), g3(bv_x, n), g3(bg_x, n), dec_lanes, st_c)
        yc_x, yd_x = _convpool(uc_x.reshape(b, n, 2 * GROUP_W), g3(ud_x, n), *mixer_params)
        flat = lambda a: a.reshape(-1, GROUP_W)
        x2, h2 = _out_proj([flat(ya_x), flat(yb_x), flat(yc_x), flat(yd_x)], w_out_bf, x2, mod_x,
                           row(norm2_g[l]), n)
        x2 = _conv_ffn(h2, wu_r, dw_r, db_r, wd_r, x2, mod_x, row(final_g), n, final=not need_ctx)

        if need_ctx:
            ya_c = _attention(g3(aq_c, nc), [g3(ak_c, nc)], [g3(av_c, nc)],
                              diff_lambda[l], diff_subln_g[l], lambda_init)
            yc_c, yd_c = _convpool(uc_c.reshape(b, nc, 2 * GROUP_W), g3(ud_c, nc), *mixer_params)
            c2, hc2 = _out_proj([flat(ya_c), flat(yb_c), flat(yc_c), flat(yd_c)], w_out_bf, c2, mod_c,
                                row(norm2_g[l]), nc)
            c2 = _conv_ffn(hc2, wu_r, dw_r, db_r, wd_r, c2, mod_c, row(final_g), nc, final=False)

    return x2.reshape(b, n, d)
```

```python
import functools
import math

import numpy as np
import jax
import jax.numpy as jnp
from jax import lax
from jax.experimental import pallas as pl
from jax.experimental.pallas import tpu as pltpu

F32 = jnp.float32
BF16 = jnp.bfloat16

D_MODEL = 1024
DEPTH = 4
GRID_W = 64
N_MIXERS = 4
GROUP_W = D_MODEL // N_MIXERS
A_HEADS = 4
A_VDIM = GROUP_W // A_HEADS
A_QKDIM = A_VDIM // 2
B_HEADS = 4
B_DIM = GROUP_W // B_HEADS
C_KSIZE = 31
POOL_WINDOWS = (2, 4, 8, 16)
POOL_GROUP = GROUP_W // len(POOL_WINDOWS)
D_FF = 2816
FFN_KSIZE = 3
RET_CHUNK = 128
ROPE_BASE = 10000.0
EPS = 1e-6
N_MOD = 6
OFF_AQ = 0
OFF_BQ = OFF_AQ + GROUP_W
OFF_BG = OFF_BQ + GROUP_W
OFF_C = OFF_BG + GROUP_W
OFF_D = OFF_C + 2 * GROUP_W
OFF_KV = OFF_D + GROUP_W
KV_AK = 0
KV_AV = GROUP_W
KV_BK = 2 * GROUP_W
KV_BV = 3 * GROUP_W
D_IN = OFF_KV + 4 * GROUP_W

LANES = 128
BF16_ROWS = 16
MOD_ROWS = 16
TOKEN_TILE = 512
ATTN_Q_TILE = 256
ATTN_KEY_TILE = 256
ATTN_TILES_PER_ITER = 3
LOG2E = 1.4426950408889634
CONV_ROW_BLOCK = 32
PAD_ROWS = 16
FFN_CHUNK = 256
VMEM_LIMIT = 56 * 1024 * 1024


def _cparams(*sem):
    return pltpu.CompilerParams(dimension_semantics=sem, vmem_limit_bytes=VMEM_LIMIT)


def _sigmoid(x):
    return 1.0 / (1.0 + jnp.exp(-x))


def _silu(x):
    return x * _sigmoid(x)


def _dot(a, b):
    return jnp.dot(a, b, preferred_element_type=F32)


def _dot_nt(a, b):
    return lax.dot_general(a, b, (((1,), (1,)), ((), ())), preferred_element_type=F32)


def _rope_tables(n, blk):
    lane = np.arange(LANES)
    freq = lane % blk
    is_a = (lane % (2 * blk)) < blk
    is_row = (lane % (4 * blk)) < 2 * blk
    inv = ROPE_BASE ** (-freq.astype(np.float64) / blk)
    t = np.arange(n)
    pos = np.where(is_row[None, :], (t // GRID_W)[:, None], (t % GRID_W)[:, None]).astype(np.float64)
    ang = pos * inv[None, :]
    cos = np.cos(ang)
    sin = np.sin(ang) * np.where(is_a, -1.0, 1.0)[None, :]
    return jnp.asarray(cos, F32), jnp.asarray(sin, F32)


def _rope(x, cos, sin, blk):
    lane = lax.broadcasted_iota(jnp.int32, (1, LANES), 1)
    is_a = (lane % (2 * blk)) < blk
    outs = []
    for j in range(GROUP_W // LANES):
        xh = x[:, j * LANES:(j + 1) * LANES]
        from_hi = pltpu.roll(xh, LANES - blk, axis=1)
        from_lo = pltpu.roll(xh, blk, axis=1)
        outs.append(xh * cos + jnp.where(is_a, from_hi, from_lo) * sin)
    return jnp.concatenate(outs, axis=1)


def _head_mean(a, width):
    r = lax.broadcasted_iota(jnp.int32, (GROUP_W, GROUP_W), 0) // width
    c = lax.broadcasted_iota(jnp.int32, (GROUP_W, GROUP_W), 1) // width
    seg = jnp.where(r == c, 1.0 / width, 0.0).astype(BF16)
    a1 = a.astype(BF16)
    r1 = a - a1.astype(F32)
    a2 = r1.astype(BF16)
    a3 = (r1 - a2.astype(F32)).astype(BF16)
    return _dot(a1, seg) + _dot(a2, seg) + _dot(a3, seg)


def _mod_kernel(s_ref, w_ref, b_ref, o_ref):
    s = _silu(s_ref[...])
    o_ref[0] = _dot(s.astype(BF16), w_ref[0].astype(BF16)) + b_ref[0]


def _modulation(cond, w_mod, b_mod):
    depth = w_mod.shape[0]
    return pl.pallas_call(
        _mod_kernel,
        grid=(depth, N_MOD),
        in_specs=[
            pl.BlockSpec((MOD_ROWS, D_MODEL), lambda l, j: (0, 0)),
            pl.BlockSpec((1, D_MODEL, D_MODEL), lambda l, j: (l, 0, j)),
            pl.BlockSpec((1, 1, D_MODEL), lambda l, j: (l, 0, j)),
        ],
        out_specs=pl.BlockSpec((1, MOD_ROWS, D_MODEL), lambda l, j: (l, 0, j)),
        out_shape=jax.ShapeDtypeStruct((depth, MOD_ROWS, N_MOD * D_MODEL), F32),
        compiler_params=_cparams("arbitrary", "arbitrary"),
        name="modulation",
    )(cond, w_mod, b_mod.reshape(depth, 1, N_MOD * D_MODEL))


def _norm_mod(x, g, shift, scale):
    ms = jnp.mean(x * x, axis=-1, keepdims=True)
    return (x * lax.rsqrt(ms + EPS) * g) * (1.0 + scale) + shift


def _in_kernel(x_ref, mod_ref, g_ref, w_ref, ca_ref, sa_ref, cb_ref, sb_ref,
               aq_ref, ak_ref, av_ref, bq_ref, bk_ref, bv_ref, bg_ref, uc_ref, ud_ref):
    h = _norm_mod(x_ref[...], g_ref[...], mod_ref[0, 0:1, :], mod_ref[0, 1:2, :]).astype(BF16)

    def proj(off, width=GROUP_W):
        return _dot(h, w_ref[:, off:off + width])

    ca, sa, cb, sb = ca_ref[...], sa_ref[...], cb_ref[...], sb_ref[...]
    aq_ref[...] = (_rope(proj(OFF_AQ), ca, sa, A_QKDIM // 4) * (A_QKDIM ** -0.5 * LOG2E)).astype(BF16)
    ak_ref[...] = _rope(proj(OFF_KV + KV_AK), ca, sa, A_QKDIM // 4).astype(BF16)
    av_ref[...] = proj(OFF_KV + KV_AV).astype(BF16)
    bq_ref[...] = _rope(proj(OFF_BQ), cb, sb, B_DIM // 4)
    bk_ref[...] = _rope(proj(OFF_KV + KV_BK), cb, sb, B_DIM // 4) * (B_DIM ** -0.5)
    bv_ref[...] = proj(OFF_KV + KV_BV).astype(BF16)
    bg_ref[...] = proj(OFF_BG)
    uc_ref[...] = proj(OFF_C, 2 * GROUP_W)
    ud_ref[...] = proj(OFF_D)


def _in_proj(x2, mod, g, w_bf, tables, seq):
    t = x2.shape[0]
    tm = min(TOKEN_TILE, seq)
    per_seq = seq // tm
    tok = lambda i: (i, 0)
    tab = pl.BlockSpec((tm, LANES), lambda i: (i % per_seq, 0))
    gw = lambda dt: jax.ShapeDtypeStruct((t, GROUP_W), dt)
    gspec = pl.BlockSpec((tm, GROUP_W), tok)
    return pl.pallas_call(
        _in_kernel,
        grid=(t // tm,),
        in_specs=[
            pl.BlockSpec((tm, D_MODEL), tok),
            pl.BlockSpec((1, N_MOD, D_MODEL), lambda i: (i // per_seq, 0, 0)),
            pl.BlockSpec((1, D_MODEL), lambda i: (0, 0)),
            pl.BlockSpec((D_MODEL, D_IN), lambda i: (0, 0)),
            tab, tab, tab, tab,
        ],
        out_specs=[gspec] * 7 + [pl.BlockSpec((tm, 2 * GROUP_W), tok), gspec],
        out_shape=[gw(BF16), gw(BF16), gw(BF16), gw(F32), gw(F32), gw(BF16), gw(F32),
                   jax.ShapeDtypeStruct((t, 2 * GROUP_W), F32), gw(F32)],
        compiler_params=_cparams("arbitrary"),
        name="in_proj",
    )(x2, mod, g, w_bf, *tables)


def _attn_kernel(lambda_init, q_ref, k_ref, v_ref, lam_ref, g_ref, o_ref, qs_scr, s_scr, mx_scr, l_scr, acc_scr):
    tq = q_ref.shape[1]
    n_maps = 2 * A_HEADS
    kt = ATTN_KEY_TILE
    n_tiles = k_ref.shape[1] // kt
    per_iter = ATTN_TILES_PER_ITER if n_tiles % ATTN_TILES_PER_ITER == 0 else 1
    lp = lam_ref[...]
    lam = (jnp.exp(jnp.sum(lp[0:1] * lp[1:2], axis=-1, keepdims=True))
           - jnp.exp(jnp.sum(lp[2:3] * lp[3:4], axis=-1, keepdims=True)) + lambda_init)
    q = q_ref[0]
    lane = lax.broadcasted_iota(jnp.int32, (1, GROUP_W), 1)
    for u in range(n_maps):
        keep = (lane >= u * A_QKDIM) & (lane < (u + 1) * A_QKDIM)
        qs_scr[u * tq:(u + 1) * tq, :] = jnp.where(keep, q, jnp.zeros_like(q))

    mx_scr[...] = jnp.full(mx_scr.shape, -jnp.inf, F32)

    def qk(j, carry):
        for t in range(per_iter):
            tile = j * per_iter + t
            r = pl.multiple_of(tile * kt, kt)
            s = _dot_nt(qs_scr[...], k_ref[0, pl.ds(r, kt), :])
            s_scr[tile] = s
            mx_scr[...] = jnp.maximum(mx_scr[...], jnp.maximum(s[:, :LANES], s[:, LANES:]))
        return carry

    lax.fori_loop(0, n_tiles // per_iter, qk, 0)
    mx_scr[...] = jnp.broadcast_to(jnp.max(mx_scr[...], axis=-1, keepdims=True), mx_scr.shape)

    l_scr[...] = jnp.zeros(l_scr.shape, F32)
    acc_scr[...] = jnp.zeros(acc_scr.shape, F32)

    def pv(j, carry):
        for t in range(per_iter):
            tile = j * per_iter + t
            r = pl.multiple_of(tile * kt, kt)
            m = mx_scr[...]
            e = jnp.exp2(s_scr[tile] - jnp.concatenate([m, m], axis=1))
            l_scr[...] = l_scr[...] + (e[:, :LANES] + e[:, LANES:])
            acc_scr[...] = acc_scr[...] + _dot(e.astype(BF16), v_ref[0, pl.ds(r, kt), :])
        return carry

    lax.fori_loop(0, n_tiles // per_iter, pv, 0)

    o = jnp.zeros((tq, GROUP_W), F32)
    for hd in range(A_HEADS):
        parts = []
        for m in range(2):
            u = 2 * hd + m
            rows = slice(u * tq, (u + 1) * tq)
            parts.append(acc_scr[rows, :] * (1.0 / jnp.sum(l_scr[rows, :], axis=-1, keepdims=True)))
        o = jnp.where((lane >= hd * A_VDIM) & (lane < (hd + 1) * A_VDIM), parts[0] - lam * parts[1], o)
    y = o * lax.rsqrt(_head_mean(o * o, A_VDIM) + EPS) * g_ref[...] * (1.0 - lambda_init)
    o_ref[0] = y.astype(BF16)


def _attention(q, k, v, lam_p, subln_g, lambda_init):
    b, nq, _ = q.shape
    nk = k.shape[1]
    tq = min(ATTN_Q_TILE, nq)
    rows = 2 * A_HEADS * tq
    full = pl.BlockSpec((1, nk, GROUP_W), lambda bi, i: (bi, 0, 0))
    return pl.pallas_call(
        functools.partial(_attn_kernel, lambda_init),
        grid=(b, nq // tq),
        in_specs=[pl.BlockSpec((1, tq, GROUP_W), lambda bi, i: (bi, i, 0)), full, full,
                  pl.BlockSpec((4, A_QKDIM), lambda bi, i: (0, 0)),
                  pl.BlockSpec((1, GROUP_W), lambda bi, i: (0, 0))],
        out_specs=pl.BlockSpec((1, tq, GROUP_W), lambda bi, i: (bi, i, 0)),
        out_shape=jax.ShapeDtypeStruct((b, nq, GROUP_W), BF16),
        scratch_shapes=[pltpu.VMEM((rows, GROUP_W), BF16),
                        pltpu.VMEM((nk // ATTN_KEY_TILE, rows, ATTN_KEY_TILE), F32),
                        pltpu.VMEM((rows, LANES), F32),
                        pltpu.VMEM((rows, LANES), F32),
                        pltpu.VMEM((rows, GROUP_W), F32)],
        compiler_params=_cparams("arbitrary", "arbitrary"),
        name="diff_attention",
    )(q, k, v, lam_p, jnp.tile(subln_g, A_HEADS).reshape(1, GROUP_W))


def _ret_kernel(q_ref, k_ref, v_ref, gate_ref, dec_ref, s0_ref, y_ref, st_ref, o_scr, s_scr):
    n = q_ref.shape[1]
    c = RET_CHUNK
    nch = n // c
    lane = lax.broadcasted_iota(jnp.int32, (1, GROUP_W), 1)
    head_masks = [(lane >= h * B_DIM) & (lane < (h + 1) * B_DIM) for h in range(B_HEADS)]
    rr = lax.broadcasted_iota(jnp.int32, (GROUP_W, GROUP_W), 0) // B_DIM
    cc = lax.broadcasted_iota(jnp.int32, (GROUP_W, GROUP_W), 1) // B_DIM
    block_diag = rr == cc
    pos = lax.broadcasted_iota(jnp.int32, (c, 1), 0).astype(F32)
    ii = lax.broadcasted_iota(jnp.int32, (c, c), 0)
    jj = lax.broadcasted_iota(jnp.int32, (c, c), 1)

    for d in range(2):
        lg = -jnp.exp(dec_ref[d:d + 1, :])
        rel = (ii - jj) if d == 0 else (jj - ii)
        relf = jnp.maximum(rel, 0).astype(F32)
        intra = jnp.concatenate(
            [jnp.where(rel >= 0, jnp.exp(relf * lg[:, h * B_DIM:h * B_DIM + 1]), 0.0) for h in range(B_HEADS)],
            axis=0)
        if d == 0:
            q_dec = jnp.exp((pos + 1.0) * lg)
            k_dec = jnp.exp((c - 1.0 - pos) * lg)
        else:
            q_dec = jnp.exp((c - pos) * lg)
            k_dec = jnp.exp(pos * lg)
        c_dec = jnp.exp(float(c) * lg)
        s_scr[...] = s0_ref[0, d]

        def chunk(i, carry):
            ci = i if d == 0 else nch - 1 - i
            r = pl.multiple_of(ci * c, c)
            qc = q_ref[0, pl.ds(r, c), :]
            kc = k_ref[0, pl.ds(r, c), :]
            vc = v_ref[0, pl.ds(r, c), :]
            qb = qc.astype(BF16)
            q4 = jnp.concatenate([jnp.where(mk, qb, jnp.zeros_like(qb)) for mk in head_masks], axis=0)
            sc = (_dot_nt(q4, kc.astype(BF16)) * intra).astype(BF16)
            state = s_scr[...]
            out = _dot((qc * q_dec).astype(BF16), state.astype(BF16))
            for h in range(B_HEADS):
                out = out + jnp.where(head_masks[h], _dot(sc[h * c:(h + 1) * c], vc), 0.0)
            if d == 0:
                o_scr[pl.ds(r, c), :] = out
            else:
                o_scr[pl.ds(r, c), :] = o_scr[pl.ds(r, c), :] + out
            kd_t = (kc * k_dec).T.astype(BF16)
            s_scr[...] = c_dec * state + jnp.where(block_diag, _dot(kd_t, vc), 0.0)
            return carry

        lax.fori_loop(0, nch, chunk, 0)
        st_ref[0, d] = s_scr[...]

    def finish(i, carry):
        r = pl.multiple_of(i * c, c)
        o = o_scr[pl.ds(r, c), :]
        y = _silu(gate_ref[0, pl.ds(r, c), :]) * (o * lax.rsqrt(_head_mean(o * o, B_DIM) + EPS))
        y_ref[0, pl.ds(r, c), :] = y.astype(BF16)
        return carry

    lax.fori_loop(0, nch, finish, 0)


def _retention(q, k, v, gate, dec_lanes, s0):
    b, n, _ = q.shape
    seq = pl.BlockSpec((1, n, GROUP_W), lambda bi: (bi, 0, 0))
    st = pl.BlockSpec((1, 2, GROUP_W, GROUP_W), lambda bi: (bi, 0, 0, 0))
    return pl.pallas_call(
        _ret_kernel,
        grid=(b,),
        in_specs=[seq, seq, seq, seq, pl.BlockSpec((2, GROUP_W), lambda bi: (0, 0)), st],
        out_specs=[seq, st],
        out_shape=[jax.ShapeDtypeStruct((b, n, GROUP_W), BF16),
                   jax.ShapeDtypeStruct((b, 2, GROUP_W, GROUP_W), F32)],
        scratch_shapes=[pltpu.VMEM((n, GROUP_W), F32), pltpu.VMEM((GROUP_W, GROUP_W), F32)],
        compiler_params=_cparams("arbitrary"),
        name="retention",
    )(q, k, v, gate, dec_lanes, s0)


def _convpool_kernel(uc_ref, ud_ref, cw_ref, cb_ref, lg_ref, lb_ref, pw_ref, ps_ref, yc_ref, yd_ref,
                     ubuf, dbuf):
    n = uc_ref.shape[1]
    rb = CONV_ROW_BLOCK
    zeros = jnp.zeros((PAD_ROWS, GROUP_W), F32)
    for buf in (ubuf, dbuf):
        buf[0:PAD_ROWS, :] = zeros
        buf[PAD_ROWS + n:PAD_ROWS + n + PAD_ROWS, :] = zeros

    def fill(i, carry):
        r = pl.multiple_of(i * RET_CHUNK, RET_CHUNK)
        u2 = uc_ref[0, pl.ds(r, RET_CHUNK), :]
        ubuf[pl.ds(PAD_ROWS + r, RET_CHUNK), :] = u2[:, :GROUP_W] * _sigmoid(u2[:, GROUP_W:])
        dbuf[pl.ds(PAD_ROWS + r, RET_CHUNK), :] = ud_ref[0, pl.ds(r, RET_CHUNK), :]
        return carry

    lax.fori_loop(0, n // RET_CHUNK, fill, 0)

    lane = lax.broadcasted_iota(jnp.int32, (1, GROUP_W), 1)
    half = jnp.left_shift(1, lane // POOL_GROUP)
    max_half = POOL_WINDOWS[-1] // 2

    def block(i, carry):
        r = pl.multiple_of(i * rb, rb)
        uwin = ubuf[pl.ds(r, rb + 2 * PAD_ROWS), :]
        dwin = dbuf[pl.ds(r, rb + 2 * PAD_ROWS), :]
        acc = jnp.zeros((rb, GROUP_W), F32) + cb_ref[...]
        for kk in range(C_KSIZE):
            s = PAD_ROWS - C_KSIZE // 2 + kk
            acc = acc + uwin[s:s + rb] * cw_ref[kk:kk + 1, :]
        mu = jnp.mean(acc, axis=-1, keepdims=True)
        var = jnp.mean(jnp.square(acc - mu), axis=-1, keepdims=True)
        yn = (acc - mu) * lax.rsqrt(var + EPS) * lg_ref[...] + lb_ref[...]
        yc_ref[0, pl.ds(r, rb), :] = _silu(yn).astype(BF16)
        tot = jnp.zeros((rb, GROUP_W), F32)
        for j in range(-max_half, max_half):
            inside = (half > j) if j >= 0 else (half >= -j)
            tot = tot + jnp.where(inside, dwin[PAD_ROWS + j:PAD_ROWS + j + rb], 0.0)
        t = r + lax.broadcasted_iota(jnp.int32, (rb, 1), 0)
        cnt = (jnp.minimum(t + half, n) - jnp.maximum(t - half, 0)).astype(F32)
        pooled = tot / cnt - dwin[PAD_ROWS:PAD_ROWS + rb]
        yd_ref[0, pl.ds(r, rb), :] = (_dot(pooled.astype(BF16), pw_ref[...]) * ps_ref[...]).astype(BF16)
        return carry

    lax.fori_loop(0, n // rb, block, 0)


def _convpool(uc, ud, cw, cb, lng, lnb, pw_bd, pscale):
    b, n, _ = ud.shape
    seq = lambda w: pl.BlockSpec((1, n, w), lambda bi: (bi, 0, 0))
    row = pl.BlockSpec((1, GROUP_W), lambda bi: (0, 0))
    return pl.pallas_call(
        _convpool_kernel,
        grid=(b,),
        in_specs=[seq(2 * GROUP_W), seq(GROUP_W),
                  pl.BlockSpec((C_KSIZE, GROUP_W), lambda bi: (0, 0)), row, row, row,
                  pl.BlockSpec((GROUP_W, GROUP_W), lambda bi: (0, 0)), row],
        out_specs=[seq(GROUP_W), seq(GROUP_W)],
        out_shape=[jax.ShapeDtypeStruct((b, n, GROUP_W), BF16)] * 2,
        scratch_shapes=[pltpu.VMEM((n + 2 * PAD_ROWS, GROUP_W), F32)] * 2,
        compiler_params=_cparams("arbitrary"),
        name="conv_pool",
    )(uc, ud, cw, cb, lng, lnb, pw_bd, pscale)


def _out_kernel(ya_ref, yb_ref, yc_ref, yd_ref, w_ref, x_ref, mod_ref, g_ref, xo_ref, h_ref):
    y = jnp.zeros(x_ref.shape, F32)
    for gi, y_ref in enumerate((ya_ref, yb_ref, yc_ref, yd_ref)):
        y = y + _dot(y_ref[...], w_ref[gi * GROUP_W:(gi + 1) * GROUP_W, :])
    x = x_ref[...] + mod_ref[0, 2:3, :] * y
    xo_ref[...] = x
    h_ref[...] = _norm_mod(x, g_ref[...], mod_ref[0, 3:4, :], mod_ref[0, 4:5, :]).astype(BF16)


def _out_proj(ys, w_bf, x2, mod, g, seq):
    t = x2.shape[0]
    tm = min(TOKEN_TILE, seq)
    per_seq = seq // tm
    tok = lambda i: (i, 0)
    gspec = pl.BlockSpec((tm, GROUP_W), tok)
    xspec = pl.BlockSpec((tm, D_MODEL), tok)
    return pl.pallas_call(
        _out_kernel,
        grid=(t // tm,),
        in_specs=[gspec] * 4 + [pl.BlockSpec((D_MODEL, D_MODEL), lambda i: (0, 0)), xspec,
                                pl.BlockSpec((1, N_MOD, D_MODEL), lambda i: (i // per_seq, 0, 0)),
                                pl.BlockSpec((1, D_MODEL), lambda i: (0, 0))],
        out_specs=[xspec, xspec],
        out_shape=[jax.ShapeDtypeStruct((t, D_MODEL), F32), jax.ShapeDtypeStruct((t, D_MODEL), BF16)],
        compiler_params=_cparams("arbitrary"),
        name="out_proj",
    )(*ys, w_bf, x2, mod, g)


def _ffn_kernel(per_seq, final, h_ref, hp_ref, hn_ref, wu_ref, dw_ref, db_ref, wd_ref, x_ref, mod_ref, fg_ref,
                o_ref, hbuf, ubuf, acc):
    tm = h_ref.shape[0]
    i = pl.program_id(0)
    first = (i % per_seq) == 0
    last = (i % per_seq) == per_seq - 1
    hp = hp_ref[...]
    hn = hn_ref[...]
    hbuf[0:BF16_ROWS, :] = jnp.where(first, jnp.zeros_like(hp), hp)
    hbuf[BF16_ROWS:BF16_ROWS + tm, :] = h_ref[...]
    hbuf[BF16_ROWS + tm:, :] = jnp.where(last, jnp.zeros_like(hn), hn)
    acc[...] = jnp.zeros_like(acc)
    tf = FFN_CHUNK

    def chunk(j, carry):
        ubuf[...] = _dot(hbuf[...], wu_ref[j])
        w = dw_ref[j]
        u = db_ref[j]
        for kk in range(FFN_KSIZE):
            u = u + ubuf[pl.ds(BF16_ROWS - FFN_KSIZE // 2 + kk, tm), :] * w[kk:kk + 1, :]
        act = (u[:, :tf] * _silu(u[:, tf:])).astype(BF16)
        acc[...] = acc[...] + _dot(act, wd_ref[j])
        return carry

    lax.fori_loop(0, wu_ref.shape[0], chunk, 0)
    x = x_ref[...] + mod_ref[0, 5:6, :] * acc[...]
    if final:
        ms = jnp.mean(x * x, axis=-1, keepdims=True)
        x = x * lax.rsqrt(ms + EPS) * fg_ref[...]
    o_ref[...] = x


def _conv_ffn(h2, wu_r, dw_r, db_r, wd_r, x2, mod, final_g, seq, final):
    t = x2.shape[0]
    tm = min(TOKEN_TILE, seq)
    per_seq = seq // tm
    nblk = tm // BF16_ROWS
    tok = lambda i: (i, 0)
    nj = wu_r.shape[0]
    const3 = lambda i: (0, 0, 0)
    halo = lambda f: pl.BlockSpec((BF16_ROWS, D_MODEL), f)
    xspec = pl.BlockSpec((tm, D_MODEL), tok)
    return pl.pallas_call(
        functools.partial(_ffn_kernel, per_seq, final),
        grid=(t // tm,),
        in_specs=[xspec,
                  halo(lambda i: (jnp.maximum(i * nblk - 1, 0), 0)),
                  halo(lambda i: (jnp.minimum((i + 1) * nblk, t // BF16_ROWS - 1), 0)),
                  pl.BlockSpec((nj, D_MODEL, 2 * FFN_CHUNK), const3),
                  pl.BlockSpec((nj, FFN_KSIZE, 2 * FFN_CHUNK), const3),
                  pl.BlockSpec((nj, 1, 2 * FFN_CHUNK), const3),
                  pl.BlockSpec((nj, FFN_CHUNK, D_MODEL), const3),
                  xspec,
                  pl.BlockSpec((1, N_MOD, D_MODEL), lambda i: (i // per_seq, 0, 0)),
                  pl.BlockSpec((1, D_MODEL), lambda i: (0, 0))],
        out_specs=xspec,
        out_shape=jax.ShapeDtypeStruct((t, D_MODEL), F32),
        scratch_shapes=[pltpu.VMEM((tm + 2 * BF16_ROWS, D_MODEL), BF16),
                        pltpu.VMEM((tm + 2 * BF16_ROWS, 2 * FFN_CHUNK), F32),
                        pltpu.VMEM((tm, D_MODEL), F32)],
        compiler_params=_cparams("arbitrary"),
        name="conv_ffn",
    )(h2, h2, h2, wu_r, dw_r, db_r, wd_r, x2, mod, final_g)


def _split_ffn(a, lead):
    nj = D_FF // FFN_CHUNK
    v = a[..., :D_FF].reshape(lead + (nj, FFN_CHUNK))
    g = a[..., D_FF:].reshape(lead + (nj, FFN_CHUNK))
    both = jnp.concatenate([v, g], axis=-1)
    return jnp.moveaxis(both, -2, 0)


def kernel(x, c, ctx, c_ctx, w_mod, b_mod, norm1_g, norm2_g, w_in, w_out, diff_lambda, diff_subln_g, ret_decay,
           conv_dw_w, conv_dw_b, conv_ln_g, conv_ln_b, pool_w, pool_scale, ffn_w_up, ffn_dw_w, ffn_dw_b,
           ffn_w_down, final_g):
    b, n, d = x.shape
    nc = ctx.shape[1]
    depth = w_mod.shape[0]
    assert d == D_MODEL and b + 1 <= MOD_ROWS
    assert n % RET_CHUNK == 0 and nc % RET_CHUNK == 0 and n % min(TOKEN_TILE, n) == 0

    cond = jnp.zeros((MOD_ROWS, D_MODEL), F32).at[:b].set(c).at[b].set(c_ctx)
    mods = _modulation(cond, w_mod, b_mod)

    tab_x = _rope_tables(n, A_QKDIM // 4) + _rope_tables(n, B_DIM // 4)
    one, zero = jnp.ones((nc, LANES), F32), jnp.zeros((nc, LANES), F32)
    tab_c = (one, zero, one, zero)

    x2 = x.reshape(b * n, d)
    c2 = ctx.reshape(b * nc, d)
    row = lambda v: v.reshape(1, -1)
    g3 = lambda a, s: a.reshape(b, s, GROUP_W)
    zero_state = jnp.zeros((b, 2, GROUP_W, GROUP_W), F32)

    for l in range(depth):
        need_ctx = l < depth - 1
        lambda_init = 0.8 - 0.6 * math.exp(-0.3 * l)
        mod_x = mods[l, :b].reshape(b, N_MOD, D_MODEL)
        mod_c = jnp.broadcast_to(mods[l, b].reshape(1, N_MOD, D_MODEL), (b, N_MOD, D_MODEL))
        w_in_bf = w_in[l].astype(BF16)
        w_out_bf = w_out[l].astype(BF16)
        wu_r = _split_ffn(ffn_w_up[l].astype(BF16), (D_MODEL,))
        dw_r = _split_ffn(ffn_dw_w[l], (FFN_KSIZE,))
        db_r = _split_ffn(ffn_dw_b[l].reshape(1, -1), (1,))
        wd_r = ffn_w_down[l].astype(BF16).reshape(D_FF // FFN_CHUNK, FFN_CHUNK, D_MODEL)
        dec_lanes = jnp.repeat(ret_decay[l], B_DIM, axis=-1)
        pw_bd = jnp.zeros((GROUP_W, GROUP_W), F32)
        for gi in range(len(POOL_WINDOWS)):
            pw_bd = pw_bd.at[gi * POOL_GROUP:(gi + 1) * POOL_GROUP, gi * POOL_GROUP:(gi + 1) * POOL_GROUP].set(
                pool_w[l, gi])
        pw_bd = pw_bd.astype(BF16)
        mixer_params = (conv_dw_w[l], row(conv_dw_b[l]), row(conv_ln_g[l]), row(conv_ln_b[l]), pw_bd,
                        row(pool_scale[l]))

        px = _in_proj(x2, mod_x, row(norm1_g[l]), w_in_bf, tab_x, n)
        pc = _in_proj(c2, mod_c, row(norm1_g[l]), w_in_bf, tab_c, nc)
        aq_x, ak_x, av_x, bq_x, bk_x, bv_x, bg_x, uc_x, ud_x = px
        aq_c, ak_c, av_c, bq_c, bk_c, bv_c, bg_c, uc_c, ud_c = pc

        ya_x = _attention(g3(aq_x, n), jnp.concatenate([g3(ak_c, nc), g3(ak_x, n)], axis=1),
                          jnp.concatenate([g3(av_c, nc), g3(av_x, n)], axis=1),
                          diff_lambda[l], diff_subln_g[l], lambda_init)
        yb_c, st_c = _retention(g3(bq_c, nc), g3(bk_c, nc), g3(bv_c, nc), g3(bg_c, nc), dec_lanes, zero_state)
        yb_x, _ = _retention(g3(bq_x, n), g3(bk_x, n), g3(bv_x, n), g3(bg_x, n), dec_lanes, st_c)
        yc_x, yd_x = _convpool(uc_x.reshape(b, n, 2 * GROUP_W), g3(ud_x, n), *mixer_params)
        flat = lambda a: a.reshape(-1, GROUP_W)
        x2, h2 = _out_proj([flat(ya_x), flat(yb_x), flat(yc_x), flat(yd_x)], w_out_bf, x2, mod_x,
                           row(norm2_g[l]), n)
        x2 = _conv_ffn(h2, wu_r, dw_r, db_r, wd_r, x2, mod_x, row(final_g), n, final=not need_ctx)

        if need_ctx:
            ya_c = _attention(g3(aq_c, nc), g3(ak_c, nc), g3(av_c, nc),
                              diff_lambda[l], diff_subln_g[l], lambda_init)
            yc_c, yd_c = _convpool(uc_c.reshape(b, nc, 2 * GROUP_W), g3(ud_c, nc), *mixer_params)
            c2, hc2 = _out_proj([flat(ya_c), flat(yb_c), flat(yc_c), flat(yd_c)], w_out_bf, c2, mod_c,
                                row(norm2_g[l]), nc)
            c2 = _conv_ffn(hc2, wu_r, dw_r, db_r, wd_r, c2, mod_c, row(final_g), nc, final=False)

    return x2.reshape(b, n, d)
```

```python
import functools
import math

import numpy as np
import jax
import jax.numpy as jnp
from jax import lax
from jax.experimental import pallas as pl
from jax.experimental.pallas import tpu as pltpu

F32 = jnp.float32
BF16 = jnp.bfloat16

D_MODEL = 1024
DEPTH = 4
GRID_W = 64
N_MIXERS = 4
GROUP_W = D_MODEL // N_MIXERS
A_HEADS = 4
A_VDIM = GROUP_W // A_HEADS
A_QKDIM = A_VDIM // 2
B_HEADS = 4
B_DIM = GROUP_W // B_HEADS
C_KSIZE = 31
POOL_WINDOWS = (2, 4, 8, 16)
POOL_GROUP = GROUP_W // len(POOL_WINDOWS)
D_FF = 2816
FFN_KSIZE = 3
RET_CHUNK = 128
ROPE_BASE = 10000.0
EPS = 1e-6
N_MOD = 6
OFF_AQ = 0
OFF_BQ = OFF_AQ + GROUP_W
OFF_BG = OFF_BQ + GROUP_W
OFF_C = OFF_BG + GROUP_W
OFF_D = OFF_C + 2 * GROUP_W
OFF_KV = OFF_D + GROUP_W
KV_AK = 0
KV_AV = GROUP_W
KV_BK = 2 * GROUP_W
KV_BV = 3 * GROUP_W
D_IN = OFF_KV + 4 * GROUP_W

LANES = 128
BF16_ROWS = 16
MOD_ROWS = 16
TOKEN_TILE = 512
ATTN_Q_TILE = 256
ATTN_KEY_TILE = 256
ATTN_TILES_PER_ITER = 3
LOG2E = 1.4426950408889634
CONV_ROW_BLOCK = 64
SUBLANES = 8
PAD_ROWS = 16
TAIL_ROWS = 32
FFN_CHUNK = 256
VMEM_LIMIT = 56 * 1024 * 1024


def _cparams(*sem):
    return pltpu.CompilerParams(dimension_semantics=sem, vmem_limit_bytes=VMEM_LIMIT)


def _sigmoid(x):
    return 1.0 / (1.0 + jnp.exp(-x))


def _silu(x):
    return x * _sigmoid(x)


def _dot(a, b):
    return jnp.dot(a, b, preferred_element_type=F32)


def _dot_nt(a, b):
    return lax.dot_general(a, b, (((1,), (1,)), ((), ())), preferred_element_type=F32)


def _rope_tables(n, blk):
    lane = np.arange(LANES)
    freq = lane % blk
    is_a = (lane % (2 * blk)) < blk
    is_row = (lane % (4 * blk)) < 2 * blk
    inv = ROPE_BASE ** (-freq.astype(np.float64) / blk)
    t = np.arange(n)
    pos = np.where(is_row[None, :], (t // GRID_W)[:, None], (t % GRID_W)[:, None]).astype(np.float64)
    ang = pos * inv[None, :]
    cos = np.cos(ang)
    sin = np.sin(ang) * np.where(is_a, -1.0, 1.0)[None, :]
    return jnp.asarray(cos, F32), jnp.asarray(sin, F32)


def _rope(x, cos, sin, blk):
    lane = lax.broadcasted_iota(jnp.int32, (1, LANES), 1)
    is_a = (lane % (2 * blk)) < blk
    outs = []
    for j in range(GROUP_W // LANES):
        xh = x[:, j * LANES:(j + 1) * LANES]
        from_hi = pltpu.roll(xh, LANES - blk, axis=1)
        from_lo = pltpu.roll(xh, blk, axis=1)
        outs.append(xh * cos + jnp.where(is_a, from_hi, from_lo) * sin)
    return jnp.concatenate(outs, axis=1)


def _head_mean(a, width):
    r = lax.broadcasted_iota(jnp.int32, (GROUP_W, GROUP_W), 0) // width
    c = lax.broadcasted_iota(jnp.int32, (GROUP_W, GROUP_W), 1) // width
    seg = jnp.where(r == c, 1.0 / width, 0.0).astype(BF16)
    a1 = a.astype(BF16)
    r1 = a - a1.astype(F32)
    a2 = r1.astype(BF16)
    a3 = (r1 - a2.astype(F32)).astype(BF16)
    return _dot(a1, seg) + _dot(a2, seg) + _dot(a3, seg)


def _mod_kernel(s_ref, w_ref, b_ref, o_ref):
    s = _silu(s_ref[...])
    o_ref[0] = _dot(s.astype(BF16), w_ref[0].astype(BF16)) + b_ref[0]


def _modulation(cond, w_mod, b_mod):
    depth = w_mod.shape[0]
    return pl.pallas_call(
        _mod_kernel,
        grid=(depth, N_MOD),
        in_specs=[
            pl.BlockSpec((MOD_ROWS, D_MODEL), lambda l, j: (0, 0)),
            pl.BlockSpec((1, D_MODEL, D_MODEL), lambda l, j: (l, 0, j)),
            pl.BlockSpec((1, 1, D_MODEL), lambda l, j: (l, 0, j)),
        ],
        out_specs=pl.BlockSpec((1, MOD_ROWS, D_MODEL), lambda l, j: (l, 0, j)),
        out_shape=jax.ShapeDtypeStruct((depth, MOD_ROWS, N_MOD * D_MODEL), F32),
        compiler_params=_cparams("arbitrary", "arbitrary"),
        name="modulation",
    )(cond, w_mod, b_mod.reshape(depth, 1, N_MOD * D_MODEL))


def _norm_mod(x, g, shift, scale):
    ms = jnp.mean(x * x, axis=-1, keepdims=True)
    return (x * lax.rsqrt(ms + EPS) * g) * (1.0 + scale) + shift


def _in_kernel(x_ref, mod_ref, g_ref, w_ref, ca_ref, sa_ref, cb_ref, sb_ref,
               aq_ref, ak_ref, av_ref, bq_ref, bk_ref, bv_ref, bg_ref, uc_ref, ud_ref):
    h = _norm_mod(x_ref[...], g_ref[...], mod_ref[0, 0:1, :], mod_ref[0, 1:2, :]).astype(BF16)

    def proj(off, width=GROUP_W):
        return _dot(h, w_ref[:, off:off + width])

    ca, sa, cb, sb = ca_ref[...], sa_ref[...], cb_ref[...], sb_ref[...]
    aq_ref[...] = (_rope(proj(OFF_AQ), ca, sa, A_QKDIM // 4) * (A_QKDIM ** -0.5 * LOG2E)).astype(BF16)
    ak_ref[...] = _rope(proj(OFF_KV + KV_AK), ca, sa, A_QKDIM // 4).astype(BF16)
    av_ref[...] = proj(OFF_KV + KV_AV).astype(BF16)
    bq_ref[...] = _rope(proj(OFF_BQ), cb, sb, B_DIM // 4)
    bk_ref[...] = _rope(proj(OFF_KV + KV_BK), cb, sb, B_DIM // 4) * (B_DIM ** -0.5)
    bv_ref[...] = proj(OFF_KV + KV_BV).astype(BF16)
    bg_ref[...] = proj(OFF_BG)
    uc_ref[...] = proj(OFF_C, 2 * GROUP_W)
    ud_ref[...] = proj(OFF_D)


def _in_proj(x2, mod, g, w_bf, tables, seq):
    t = x2.shape[0]
    tm = min(TOKEN_TILE, seq)
    per_seq = seq // tm
    tok = lambda i: (i, 0)
    tab = pl.BlockSpec((tm, LANES), lambda i: (i % per_seq, 0))
    gw = lambda dt: jax.ShapeDtypeStruct((t, GROUP_W), dt)
    gspec = pl.BlockSpec((tm, GROUP_W), tok)
    return pl.pallas_call(
        _in_kernel,
        grid=(t // tm,),
        in_specs=[
            pl.BlockSpec((tm, D_MODEL), tok),
            pl.BlockSpec((1, N_MOD, D_MODEL), lambda i: (i // per_seq, 0, 0)),
            pl.BlockSpec((1, D_MODEL), lambda i: (0, 0)),
            pl.BlockSpec((D_MODEL, D_IN), lambda i: (0, 0)),
            tab, tab, tab, tab,
        ],
        out_specs=[gspec] * 7 + [pl.BlockSpec((tm, 2 * GROUP_W), tok), gspec],
        out_shape=[gw(BF16), gw(BF16), gw(BF16), gw(F32), gw(F32), gw(BF16), gw(F32),
                   jax.ShapeDtypeStruct((t, 2 * GROUP_W), F32), gw(F32)],
        compiler_params=_cparams("arbitrary"),
        name="in_proj",
    )(x2, mod, g, w_bf, *tables)


def _attn_kernel(lambda_init, q_ref, k_ref, v_ref, lam_ref, g_ref, o_ref, qs_scr, s_scr, mx_scr, l_scr, acc_scr):
    tq = q_ref.shape[1]
    n_maps = 2 * A_HEADS
    kt = ATTN_KEY_TILE
    n_tiles = k_ref.shape[1] // kt
    per_iter = ATTN_TILES_PER_ITER if n_tiles % ATTN_TILES_PER_ITER == 0 else 1
    lp = lam_ref[...]
    lam = (jnp.exp(jnp.sum(lp[0:1] * lp[1:2], axis=-1, keepdims=True))
           - jnp.exp(jnp.sum(lp[2:3] * lp[3:4], axis=-1, keepdims=True)) + lambda_init)
    q = q_ref[0]
    lane = lax.broadcasted_iota(jnp.int32, (1, GROUP_W), 1)
    for u in range(n_maps):
        keep = (lane >= u * A_QKDIM) & (lane < (u + 1) * A_QKDIM)
        qs_scr[u * tq:(u + 1) * tq, :] = jnp.where(keep, q, jnp.zeros_like(q))

    mx_scr[...] = jnp.full(mx_scr.shape, -jnp.inf, F32)

    def qk(j, carry):
        for t in range(per_iter):
            tile = j * per_iter + t
            r = pl.multiple_of(tile * kt, kt)
            s = _dot_nt(qs_scr[...], k_ref[0, pl.ds(r, kt), :])
            s_scr[tile] = s
            mx_scr[...] = jnp.maximum(mx_scr[...], jnp.maximum(s[:, :LANES], s[:, LANES:]))
        return carry

    lax.fori_loop(0, n_tiles // per_iter, qk, 0)
    mx_scr[...] = jnp.broadcast_to(jnp.max(mx_scr[...], axis=-1, keepdims=True), mx_scr.shape)

    l_scr[...] = jnp.zeros(l_scr.shape, F32)
    acc_scr[...] = jnp.zeros(acc_scr.shape, F32)

    def pv(j, carry):
        for t in range(per_iter):
            tile = j * per_iter + t
            r = pl.multiple_of(tile * kt, kt)
            m = mx_scr[...]
            e = jnp.exp2(s_scr[tile] - jnp.concatenate([m, m], axis=1))
            l_scr[...] = l_scr[...] + (e[:, :LANES] + e[:, LANES:])
            acc_scr[...] = acc_scr[...] + _dot(e.astype(BF16), v_ref[0, pl.ds(r, kt), :])
        return carry

    lax.fori_loop(0, n_tiles // per_iter, pv, 0)

    o = jnp.zeros((tq, GROUP_W), F32)
    for hd in range(A_HEADS):
        parts = []
        for m in range(2):
            u = 2 * hd + m
            rows = slice(u * tq, (u + 1) * tq)
            parts.append(acc_scr[rows, :] * (1.0 / jnp.sum(l_scr[rows, :], axis=-1, keepdims=True)))
        o = jnp.where((lane >= hd * A_VDIM) & (lane < (hd + 1) * A_VDIM), parts[0] - lam * parts[1], o)
    y = o * lax.rsqrt(_head_mean(o * o, A_VDIM) + EPS) * g_ref[...] * (1.0 - lambda_init)
    o_ref[0] = y.astype(BF16)


def _attention(q, k, v, lam_p, subln_g, lambda_init):
    b, nq, _ = q.shape
    nk = k.shape[1]
    tq = min(ATTN_Q_TILE, nq)
    rows = 2 * A_HEADS * tq
    full = pl.BlockSpec((1, nk, GROUP_W), lambda bi, i: (bi, 0, 0))
    return pl.pallas_call(
        functools.partial(_attn_kernel, lambda_init),
        grid=(b, nq // tq),
        in_specs=[pl.BlockSpec((1, tq, GROUP_W), lambda bi, i: (bi, i, 0)), full, full,
                  pl.BlockSpec((4, A_QKDIM), lambda bi, i: (0, 0)),
                  pl.BlockSpec((1, GROUP_W), lambda bi, i: (0, 0))],
        out_specs=pl.BlockSpec((1, tq, GROUP_W), lambda bi, i: (bi, i, 0)),
        out_shape=jax.ShapeDtypeStruct((b, nq, GROUP_W), BF16),
        scratch_shapes=[pltpu.VMEM((rows, GROUP_W), BF16),
                        pltpu.VMEM((nk // ATTN_KEY_TILE, rows, ATTN_KEY_TILE), F32),
                        pltpu.VMEM((rows, LANES), F32),
                        pltpu.VMEM((rows, LANES), F32),
                        pltpu.VMEM((rows, GROUP_W), F32)],
        compiler_params=_cparams("arbitrary", "arbitrary"),
        name="diff_attention",
    )(q, k, v, lam_p, jnp.tile(subln_g, A_HEADS).reshape(1, GROUP_W))


def _ret_kernel(q_ref, k_ref, v_ref, gate_ref, dec_ref, s0_ref, y_ref, st_ref, o_scr, s_scr, intra_scr, dec_scr):
    n = q_ref.shape[1]
    c = RET_CHUNK
    nch = n // c
    lane = lax.broadcasted_iota(jnp.int32, (1, GROUP_W), 1)
    head_masks = [(lane >= h * B_DIM) & (lane < (h + 1) * B_DIM) for h in range(B_HEADS)]
    rr = lax.broadcasted_iota(jnp.int32, (GROUP_W, GROUP_W), 0) // B_DIM
    cc = lax.broadcasted_iota(jnp.int32, (GROUP_W, GROUP_W), 1) // B_DIM
    block_diag = rr == cc
    pos = lax.broadcasted_iota(jnp.int32, (c, 1), 0).astype(F32)
    ii = lax.broadcasted_iota(jnp.int32, (c, c), 0)
    jj = lax.broadcasted_iota(jnp.int32, (c, c), 1)

    c_decs = []
    for d in range(2):
        lg = -jnp.exp(dec_ref[d:d + 1, :])
        rel = (ii - jj) if d == 0 else (jj - ii)
        relf = jnp.maximum(rel, 0).astype(F32)
        for h in range(B_HEADS):
            intra_scr[d, h * c:(h + 1) * c, :] = jnp.where(
                rel >= 0, jnp.exp(relf * lg[:, h * B_DIM:h * B_DIM + 1]), 0.0)
        if d == 0:
            dec_scr[d, 0] = jnp.exp((pos + 1.0) * lg)
            dec_scr[d, 1] = jnp.exp((c - 1.0 - pos) * lg)
        else:
            dec_scr[d, 0] = jnp.exp((c - pos) * lg)
            dec_scr[d, 1] = jnp.exp(pos * lg)
        c_decs.append(jnp.exp(float(c) * lg))
        s_scr[d] = s0_ref[0, d]

    def chunk(i, carry):
        for d in range(2):
            ci = i if d == 0 else nch - 1 - i
            r = pl.multiple_of(ci * c, c)
            qc = q_ref[0, pl.ds(r, c), :]
            kc = k_ref[0, pl.ds(r, c), :]
            vc = v_ref[0, pl.ds(r, c), :]
            qb = qc.astype(BF16)
            q4 = jnp.concatenate([jnp.where(mk, qb, jnp.zeros_like(qb)) for mk in head_masks], axis=0)
            sc = (_dot_nt(q4, kc.astype(BF16)) * intra_scr[d]).astype(BF16)
            state = s_scr[d]
            out = _dot((qc * dec_scr[d, 0]).astype(BF16), state.astype(BF16))
            for h in range(B_HEADS):
                out = out + jnp.where(head_masks[h], _dot(sc[h * c:(h + 1) * c], vc), 0.0)
            o_scr[d, pl.ds(r, c), :] = out
            kd_t = (kc * dec_scr[d, 1]).T.astype(BF16)
            s_scr[d] = c_decs[d] * state + jnp.where(block_diag, _dot(kd_t, vc), 0.0)
        return carry

    lax.fori_loop(0, nch, chunk, 0)
    st_ref[0] = s_scr[...]

    def finish(i, carry):
        r = pl.multiple_of(i * c, c)
        o = o_scr[0, pl.ds(r, c), :] + o_scr[1, pl.ds(r, c), :]
        y = _silu(gate_ref[0, pl.ds(r, c), :]) * (o * lax.rsqrt(_head_mean(o * o, B_DIM) + EPS))
        y_ref[0, pl.ds(r, c), :] = y.astype(BF16)
        return carry

    lax.fori_loop(0, nch, finish, 0)


def _retention(q, k, v, gate, dec_lanes, s0):
    b, n, _ = q.shape
    seq = pl.BlockSpec((1, n, GROUP_W), lambda bi: (bi, 0, 0))
    st = pl.BlockSpec((1, 2, GROUP_W, GROUP_W), lambda bi: (bi, 0, 0, 0))
    return pl.pallas_call(
        _ret_kernel,
        grid=(b,),
        in_specs=[seq, seq, seq, seq, pl.BlockSpec((2, GROUP_W), lambda bi: (0, 0)), st],
        out_specs=[seq, st],
        out_shape=[jax.ShapeDtypeStruct((b, n, GROUP_W), BF16),
                   jax.ShapeDtypeStruct((b, 2, GROUP_W, GROUP_W), F32)],
        scratch_shapes=[pltpu.VMEM((2, n, GROUP_W), F32), pltpu.VMEM((2, GROUP_W, GROUP_W), F32),
                        pltpu.VMEM((2, B_HEADS * RET_CHUNK, RET_CHUNK), F32),
                        pltpu.VMEM((2, 2, RET_CHUNK, GROUP_W), F32)],
        compiler_params=_cparams("arbitrary"),
        name="retention",
    )(q, k, v, gate, dec_lanes, s0)


def _convpool_kernel(uc_ref, ud_ref, cw_ref, cb_ref, lg_ref, lb_ref, pw_ref, ps_ref, yc_ref, yd_ref,
                     ubuf, dbuf, shift_scr):
    n = uc_ref.shape[1]
    rb = CONV_ROW_BLOCK
    for buf in (ubuf, dbuf):
        buf[0:PAD_ROWS, :] = jnp.zeros((PAD_ROWS, GROUP_W), F32)
        buf[PAD_ROWS + n:, :] = jnp.zeros((TAIL_ROWS, GROUP_W), F32)

    def fill(i, carry):
        r = pl.multiple_of(i * RET_CHUNK, RET_CHUNK)
        u2 = uc_ref[0, pl.ds(r, RET_CHUNK), :]
        ubuf[pl.ds(PAD_ROWS + r, RET_CHUNK), :] = u2[:, :GROUP_W] * _sigmoid(u2[:, GROUP_W:])
        dbuf[pl.ds(PAD_ROWS + r, RET_CHUNK), :] = ud_ref[0, pl.ds(r, RET_CHUNK), :]
        return carry

    lax.fori_loop(0, n // RET_CHUNK, fill, 0)

    lane = lax.broadcasted_iota(jnp.int32, (1, GROUP_W), 1)
    half = jnp.left_shift(1, lane // POOL_GROUP)
    first_tap = PAD_ROWS - C_KSIZE // 2

    def block(i, carry):
        r = pl.multiple_of(i * rb, rb)
        uwin = ubuf[pl.ds(r, rb + 2 * PAD_ROWS), :]
        dwin = dbuf[pl.ds(r, rb + PAD_ROWS + TAIL_ROWS), :]
        acc = jnp.zeros((rb, GROUP_W), F32) + cb_ref[...]
        for ph in range(SUBLANES):
            shift_scr[ph] = uwin[ph:ph + rb + 2 * PAD_ROWS - SUBLANES]
            for a8 in range(0, 2 * PAD_ROWS, SUBLANES):
                kk = a8 + ph - first_tap
                if 0 <= kk < C_KSIZE:
                    acc = acc + shift_scr[ph, a8:a8 + rb, :] * cw_ref[kk:kk + 1, :]
        mu = jnp.mean(acc, axis=-1, keepdims=True)
        var = jnp.mean(jnp.square(acc - mu), axis=-1, keepdims=True)
        yn = (acc - mu) * lax.rsqrt(var + EPS) * lg_ref[...] + lb_ref[...]
        yc_ref[0, pl.ds(r, rb), :] = _silu(yn).astype(BF16)
        d0 = dwin[:, :LANES]
        d1 = dwin[:, LANES:]
        p = PAD_ROWS
        s2 = d0[p - 1:p - 1 + rb] + d0[p:p + rb]
        s4 = s2 + d0[p - 2:p - 2 + rb] + d0[p + 1:p + 1 + rb]
        q2 = d1[0:rb + 32] + d1[1:rb + 33]
        q4 = q2[0:rb + 24] + q2[2:rb + 26]
        q8 = q4[0:rb + 16] + q4[4:rb + 20]
        s8 = q8[p - 4:p - 4 + rb]
        s16 = q8[p - 8:p - 8 + rb] + q8[p:p + rb]
        lane1 = lane[:, :LANES]
        tot = jnp.concatenate([jnp.where(lane1 < POOL_GROUP, s2, s4), jnp.where(lane1 < POOL_GROUP, s8, s16)],
                              axis=1)
        t = r + lax.broadcasted_iota(jnp.int32, (rb, 1), 0)
        cnt = (jnp.minimum(t + half, n) - jnp.maximum(t - half, 0)).astype(F32)
        pooled = tot / cnt - dwin[p:p + rb]
        yd_ref[0, pl.ds(r, rb), :] = (_dot(pooled.astype(BF16), pw_ref[...]) * ps_ref[...]).astype(BF16)
        return carry

    lax.fori_loop(0, n // rb, block, 0)


def _convpool(uc, ud, cw, cb, lng, lnb, pw_bd, pscale):
    b, n, _ = ud.shape
    seq = lambda w: pl.BlockSpec((1, n, w), lambda bi: (bi, 0, 0))
    row = pl.BlockSpec((1, GROUP_W), lambda bi: (0, 0))
    return pl.pallas_call(
        _convpool_kernel,
        grid=(b,),
        in_specs=[seq(2 * GROUP_W), seq(GROUP_W),
                  pl.BlockSpec((C_KSIZE, GROUP_W), lambda bi: (0, 0)), row, row, row,
                  pl.BlockSpec((GROUP_W, GROUP_W), lambda bi: (0, 0)), row],
        out_specs=[seq(GROUP_W), seq(GROUP_W)],
        out_shape=[jax.ShapeDtypeStruct((b, n, GROUP_W), BF16)] * 2,
        scratch_shapes=[pltpu.VMEM((n + PAD_ROWS + TAIL_ROWS, GROUP_W), F32)] * 2
                       + [pltpu.VMEM((SUBLANES, CONV_ROW_BLOCK + 2 * PAD_ROWS - SUBLANES, GROUP_W), F32)],
        compiler_params=_cparams("arbitrary"),
        name="conv_pool",
    )(uc, ud, cw, cb, lng, lnb, pw_bd, pscale)


def _out_kernel(ya_ref, yb_ref, yc_ref, yd_ref, w_ref, x_ref, mod_ref, g_ref, xo_ref, h_ref):
    y = jnp.zeros(x_ref.shape, F32)
    for gi, y_ref in enumerate((ya_ref, yb_ref, yc_ref, yd_ref)):
        y = y + _dot(y_ref[...], w_ref[gi * GROUP_W:(gi + 1) * GROUP_W, :])
    x = x_ref[...] + mod_ref[0, 2:3, :] * y
    xo_ref[...] = x
    h_ref[...] = _norm_mod(x, g_ref[...], mod_ref[0, 3:4, :], mod_ref[0, 4:5, :]).astype(BF16)


def _out_proj(ys, w_bf, x2, mod, g, seq):
    t = x2.shape[0]
    tm = min(TOKEN_TILE, seq)
    per_seq = seq // tm
    tok = lambda i: (i, 0)
    gspec = pl.BlockSpec((tm, GROUP_W), tok)
    xspec = pl.BlockSpec((tm, D_MODEL), tok)
    return pl.pallas_call(
        _out_kernel,
        grid=(t // tm,),
        in_specs=[gspec] * 4 + [pl.BlockSpec((D_MODEL, D_MODEL), lambda i: (0, 0)), xspec,
                                pl.BlockSpec((1, N_MOD, D_MODEL), lambda i: (i // per_seq, 0, 0)),
                                pl.BlockSpec((1, D_MODEL), lambda i: (0, 0))],
        out_specs=[xspec, xspec],
        out_shape=[jax.ShapeDtypeStruct((t, D_MODEL), F32), jax.ShapeDtypeStruct((t, D_MODEL), BF16)],
        compiler_params=_cparams("arbitrary"),
        name="out_proj",
    )(*ys, w_bf, x2, mod, g)


def _ffn_kernel(per_seq, final, h_ref, hp_ref, hn_ref, wu_ref, dw_ref, db_ref, wd_ref, x_ref, mod_ref, fg_ref,
                o_ref, hbuf, ubuf_a, ubuf_b, acc):
    tm = h_ref.shape[0]
    i = pl.program_id(0)
    first = (i % per_seq) == 0
    last = (i % per_seq) == per_seq - 1
    hp = hp_ref[...]
    hn = hn_ref[...]
    hbuf[0:BF16_ROWS, :] = jnp.where(first, jnp.zeros_like(hp), hp)
    hbuf[BF16_ROWS:BF16_ROWS + tm, :] = h_ref[...]
    hbuf[BF16_ROWS + tm:, :] = jnp.where(last, jnp.zeros_like(hn), hn)
    acc[...] = jnp.zeros_like(acc)
    tf = FFN_CHUNK
    nj = wu_ref.shape[0]

    def up(j, buf):
        buf[...] = _dot(hbuf[...], wu_ref[j])

    def mix(j, buf):
        w = dw_ref[j]
        u = db_ref[j]
        for kk in range(FFN_KSIZE):
            u = u + buf[pl.ds(BF16_ROWS - FFN_KSIZE // 2 + kk, tm), :] * w[kk:kk + 1, :]
        act = (u[:, :tf] * _silu(u[:, tf:])).astype(BF16)
        acc[...] = acc[...] + _dot(act, wd_ref[j])

    up(0, ubuf_a)

    def pair(i, carry):
        up(2 * i + 1, ubuf_b)
        mix(2 * i, ubuf_a)
        up(2 * i + 2, ubuf_a)
        mix(2 * i + 1, ubuf_b)
        return carry

    lax.fori_loop(0, (nj - 1) // 2, pair, 0)
    mix(nj - 1, ubuf_a)
    x = x_ref[...] + mod_ref[0, 5:6, :] * acc[...]
    if final:
        ms = jnp.mean(x * x, axis=-1, keepdims=True)
        x = x * lax.rsqrt(ms + EPS) * fg_ref[...]
    o_ref[...] = x


def _conv_ffn(h2, wu_r, dw_r, db_r, wd_r, x2, mod, final_g, seq, final):
    t = x2.shape[0]
    tm = min(TOKEN_TILE, seq)
    per_seq = seq // tm
    nblk = tm // BF16_ROWS
    tok = lambda i: (i, 0)
    nj = wu_r.shape[0]
    assert nj % 2 == 1
    const3 = lambda i: (0, 0, 0)
    halo = lambda f: pl.BlockSpec((BF16_ROWS, D_MODEL), f)
    xspec = pl.BlockSpec((tm, D_MODEL), tok)
    return pl.pallas_call(
        functools.partial(_ffn_kernel, per_seq, final),
        grid=(t // tm,),
        in_specs=[xspec,
                  halo(lambda i: (jnp.maximum(i * nblk - 1, 0), 0)),
                  halo(lambda i: (jnp.minimum((i + 1) * nblk, t // BF16_ROWS - 1), 0)),
                  pl.BlockSpec((nj, D_MODEL, 2 * FFN_CHUNK), const3),
                  pl.BlockSpec((nj, FFN_KSIZE, 2 * FFN_CHUNK), const3),
                  pl.BlockSpec((nj, 1, 2 * FFN_CHUNK), const3),
                  pl.BlockSpec((nj, FFN_CHUNK, D_MODEL), const3),
                  xspec,
                  pl.BlockSpec((1, N_MOD, D_MODEL), lambda i: (i // per_seq, 0, 0)),
                  pl.BlockSpec((1, D_MODEL), lambda i: (0, 0))],
        out_specs=xspec,
        out_shape=jax.ShapeDtypeStruct((t, D_MODEL), F32),
        scratch_shapes=[pltpu.VMEM((tm + 2 * BF16_ROWS, D_MODEL), BF16),
                        pltpu.VMEM((tm + 2 * BF16_ROWS, 2 * FFN_CHUNK), F32),
                        pltpu.VMEM((tm + 2 * BF16_ROWS, 2 * FFN_CHUNK), F32),
                        pltpu.VMEM((tm, D_MODEL), F32)],
        compiler_params=_cparams("arbitrary"),
        name="conv_ffn",
    )(h2, h2, h2, wu_r, dw_r, db_r, wd_r, x2, mod, final_g)


def _split_ffn(a, lead):
    nj = D_FF // FFN_CHUNK
    v = a[..., :D_FF].reshape(lead + (nj, FFN_CHUNK))
    g = a[..., D_FF:].reshape(lead + (nj, FFN_CHUNK))
    both = jnp.concatenate([v, g], axis=-1)
    return jnp.moveaxis(both, -2, 0)


def kernel(x, c, ctx, c_ctx, w_mod, b_mod, norm1_g, norm2_g, w_in, w_out, diff_lambda, diff_subln_g, ret_decay,
           conv_dw_w, conv_dw_b, conv_ln_g, conv_ln_b, pool_w, pool_scale, ffn_w_up, ffn_dw_w, ffn_dw_b,
           ffn_w_down, final_g):
    b, n, d = x.shape
    nc = ctx.shape[1]
    depth = w_mod.shape[0]
    assert d == D_MODEL and b + 1 <= MOD_ROWS
    assert n % RET_CHUNK == 0 and nc % RET_CHUNK == 0 and n % min(TOKEN_TILE, n) == 0

    cond = jnp.zeros((MOD_ROWS, D_MODEL), F32).at[:b].set(c).at[b].set(c_ctx)
    mods = _modulation(cond, w_mod, b_mod)

    tab_x = _rope_tables(n, A_QKDIM // 4) + _rope_tables(n, B_DIM // 4)
    one, zero = jnp.ones((nc, LANES), F32), jnp.zeros((nc, LANES), F32)
    tab_c = (one, zero, one, zero)

    x2 = x.reshape(b * n, d)
    c2 = ctx.reshape(b * nc, d)
    row = lambda v: v.reshape(1, -1)
    g3 = lambda a, s: a.reshape(b, s, GROUP_W)
    zero_state = jnp.zeros((b, 2, GROUP_W, GROUP_W), F32)

    for l in range(depth):
        need_ctx = l < depth - 1
        lambda_init = 0.8 - 0.6 * math.exp(-0.3 * l)
        mod_x = mods[l, :b].reshape(b, N_MOD, D_MODEL)
        mod_c = jnp.broadcast_to(mods[l, b].reshape(1, N_MOD, D_MODEL), (b, N_MOD, D_MODEL))
        w_in_bf = w_in[l].astype(BF16)
        w_out_bf = w_out[l].astype(BF16)
        wu_r = _split_ffn(ffn_w_up[l].astype(BF16), (D_MODEL,))
        dw_r = _split_ffn(ffn_dw_w[l], (FFN_KSIZE,))
        db_r = _split_ffn(ffn_dw_b[l].reshape(1, -1), (1,))
        wd_r = ffn_w_down[l].astype(BF16).reshape(D_FF // FFN_CHUNK, FFN_CHUNK, D_MODEL)
        dec_lanes = jnp.repeat(ret_decay[l], B_DIM, axis=-1)
        pw_bd = jnp.zeros((GROUP_W, GROUP_W), F32)
        for gi in range(len(POOL_WINDOWS)):
            pw_bd = pw_bd.at[gi * POOL_GROUP:(gi + 1) * POOL_GROUP, gi * POOL_GROUP:(gi + 1) * POOL_GROUP].set(
                pool_w[l, gi])
        pw_bd = pw_bd.astype(BF16)
        mixer_params = (conv_dw_w[l], row(conv_dw_b[l]), row(conv_ln_g[l]), row(conv_ln_b[l]), pw_bd,
                        row(pool_scale[l]))

        px = _in_proj(x2, mod_x, row(norm1_g[l]), w_in_bf, tab_x, n)
        pc = _in_proj(c2, mod_c, row(norm1_g[l]), w_in_bf, tab_c, nc)
        aq_x, ak_x, av_x, bq_x, bk_x, bv_x, bg_x, uc_x, ud_x = px
        aq_c, ak_c, av_c, bq_c, bk_c, bv_c, bg_c, uc_c, ud_c = pc

        ya_x = _attention(g3(aq_x, n), jnp.concatenate([g3(ak_c, nc), g3(ak_x, n)], axis=1),
                          jnp.concatenate([g3(av_c, nc), g3(av_x, n)], axis=1),
                          diff_lambda[l], diff_subln_g[l], lambda_init)
        yb_c, st_c = _retention(g3(bq_c, nc), g3(bk_c, nc), g3(bv_c, nc), g3(bg_c, nc), dec_lanes, zero_state)
        yb_x, _ = _retention(g3(bq_x, n), g3(bk_x, n), g3(bv_x, n), g3(bg_x, n), dec_lanes, st_c)
        yc_x, yd_x = _convpool(uc_x.reshape(b, n, 2 * GROUP_W), g3(ud_x, n), *mixer_params)
        flat = lambda a: a.reshape(-1, GROUP_W)
        x2, h2 = _out_proj([flat(ya_x), flat(yb_x), flat(yc_x), flat(yd_x)], w_out_bf, x2, mod_x,
                           row(norm2_g[l]), n)
        x2 = _conv_ffn(h2, wu_r, dw_r, db_r, wd_r, x2, mod_x, row(final_g), n, final=not need_ctx)

        if need_ctx:
            ya_c = _attention(g3(aq_c, nc), g3(ak_c, nc), g3(av_c, nc),
                              diff_lambda[l], diff_subln_g[l], lambda_init)
            yc_c, yd_c = _convpool(uc_c.reshape(b, nc, 2 * GROUP_W), g3(ud_c, nc), *mixer_params)
            c2, hc2 = _out_proj([flat(ya_c), flat(yb_c), flat(yc_c), flat(yd_c)], w_out_bf, c2, mod_c,
                                row(norm2_g[l]), nc)
            c2 = _conv_ffn(hc2, wu_r, dw_r, db_r, wd_r, c2, mod_c, row(final_g), nc, final=False)

    return x2.reshape(b, n, d)
```

```python
import functools
import math

import numpy as np
import jax
import jax.numpy as jnp
from jax import lax
from jax.experimental import pallas as pl
from jax.experimental.pallas import tpu as pltpu

F32 = jnp.float32
BF16 = jnp.bfloat16

D_MODEL = 1024
DEPTH = 4
GRID_W = 64
N_MIXERS = 4
GROUP_W = D_MODEL // N_MIXERS
A_HEADS = 4
A_VDIM = GROUP_W // A_HEADS
A_QKDIM = A_VDIM // 2
B_HEADS = 4
B_DIM = GROUP_W // B_HEADS
C_KSIZE = 31
POOL_WINDOWS = (2, 4, 8, 16)
POOL_GROUP = GROUP_W // len(POOL_WINDOWS)
D_FF = 2816
FFN_KSIZE = 3
RET_CHUNK = 128
ROPE_BASE = 10000.0
EPS = 1e-6
N_MOD = 6
OFF_AQ = 0
OFF_BQ = OFF_AQ + GROUP_W
OFF_BG = OFF_BQ + GROUP_W
OFF_C = OFF_BG + GROUP_W
OFF_D = OFF_C + 2 * GROUP_W
OFF_KV = OFF_D + GROUP_W
KV_AK = 0
KV_AV = GROUP_W
KV_BK = 2 * GROUP_W
KV_BV = 3 * GROUP_W
D_IN = OFF_KV + 4 * GROUP_W

LANES = 128
BF16_ROWS = 16
MOD_ROWS = 16
TOKEN_TILE = 512
ATTN_Q_TILE = 256
ATTN_KEY_TILE = 256
ATTN_TILES_PER_ITER = 3
LOG2E = 1.4426950408889634
CONV_ROW_BLOCK = 128
SUBLANES = 8
PAD_ROWS = 16
TAIL_ROWS = 32
FFN_CHUNK = 256
VMEM_LIMIT = 56 * 1024 * 1024


def _cparams(*sem):
    return pltpu.CompilerParams(dimension_semantics=sem, vmem_limit_bytes=VMEM_LIMIT)


def _sigmoid(x):
    return 1.0 / (1.0 + jnp.exp(-x))


def _silu(x):
    return x * _sigmoid(x)


def _dot(a, b):
    return jnp.dot(a, b, preferred_element_type=F32)


def _dot_nt(a, b):
    return lax.dot_general(a, b, (((1,), (1,)), ((), ())), preferred_element_type=F32)


def _rope_tables(n, blk):
    lane = np.arange(LANES)
    freq = lane % blk
    is_a = (lane % (2 * blk)) < blk
    is_row = (lane % (4 * blk)) < 2 * blk
    inv = ROPE_BASE ** (-freq.astype(np.float64) / blk)
    t = np.arange(n)
    pos = np.where(is_row[None, :], (t // GRID_W)[:, None], (t % GRID_W)[:, None]).astype(np.float64)
    ang = pos * inv[None, :]
    cos = np.cos(ang)
    sin = np.sin(ang) * np.where(is_a, -1.0, 1.0)[None, :]
    return jnp.asarray(cos, F32), jnp.asarray(sin, F32)


def _rope(x, cos, sin, blk):
    lane = lax.broadcasted_iota(jnp.int32, (1, LANES), 1)
    is_a = (lane % (2 * blk)) < blk
    outs = []
    for j in range(GROUP_W // LANES):
        xh = x[:, j * LANES:(j + 1) * LANES]
        from_hi = pltpu.roll(xh, LANES - blk, axis=1)
        from_lo = pltpu.roll(xh, blk, axis=1)
        outs.append(xh * cos + jnp.where(is_a, from_hi, from_lo) * sin)
    return jnp.concatenate(outs, axis=1)


def _head_mean(a, width):
    r = lax.broadcasted_iota(jnp.int32, (GROUP_W, GROUP_W), 0) // width
    c = lax.broadcasted_iota(jnp.int32, (GROUP_W, GROUP_W), 1) // width
    seg = jnp.where(r == c, 1.0 / width, 0.0).astype(BF16)
    a1 = a.astype(BF16)
    r1 = a - a1.astype(F32)
    a2 = r1.astype(BF16)
    a3 = (r1 - a2.astype(F32)).astype(BF16)
    return _dot(a1, seg) + _dot(a2, seg) + _dot(a3, seg)


def _mod_kernel(s_ref, w_ref, b_ref, o_ref):
    s = _silu(s_ref[...])
    o_ref[0] = _dot(s.astype(BF16), w_ref[0].astype(BF16)) + b_ref[0]


def _modulation(cond, w_mod, b_mod):
    depth = w_mod.shape[0]
    return pl.pallas_call(
        _mod_kernel,
        grid=(depth, N_MOD),
        in_specs=[
            pl.BlockSpec((MOD_ROWS, D_MODEL), lambda l, j: (0, 0)),
            pl.BlockSpec((1, D_MODEL, D_MODEL), lambda l, j: (l, 0, j)),
            pl.BlockSpec((1, 1, D_MODEL), lambda l, j: (l, 0, j)),
        ],
        out_specs=pl.BlockSpec((1, MOD_ROWS, D_MODEL), lambda l, j: (l, 0, j)),
        out_shape=jax.ShapeDtypeStruct((depth, MOD_ROWS, N_MOD * D_MODEL), F32),
        compiler_params=_cparams("arbitrary", "arbitrary"),
        name="modulation",
    )(cond, w_mod, b_mod.reshape(depth, 1, N_MOD * D_MODEL))


def _norm_mod(x, g, shift, scale):
    ms = jnp.mean(x * x, axis=-1, keepdims=True)
    return (x * lax.rsqrt(ms + EPS) * g) * (1.0 + scale) + shift


def _in_kernel(x_ref, mod_ref, g_ref, w_ref, ca_ref, sa_ref, cb_ref, sb_ref, *refs):
    aq_ref, ak_ref, av_ref, bq_ref, bk_ref, bv_ref, bg_ref, uc_ref, ud_ref = refs[-9:]
    h = _norm_mod(x_ref[...], g_ref[...], mod_ref[0, 0:1, :], mod_ref[0, 1:2, :]).astype(BF16)

    def proj(off, width=GROUP_W):
        return _dot(h, w_ref[:, off:off + width])

    ca, sa, cb, sb = ca_ref[...], sa_ref[...], cb_ref[...], sb_ref[...]
    aq_ref[...] = (_rope(proj(OFF_AQ), ca, sa, A_QKDIM // 4) * (A_QKDIM ** -0.5 * LOG2E)).astype(BF16)
    ak_ref[0] = _rope(proj(OFF_KV + KV_AK), ca, sa, A_QKDIM // 4).astype(BF16)
    av_ref[0] = proj(OFF_KV + KV_AV).astype(BF16)
    bq_ref[...] = _rope(proj(OFF_BQ), cb, sb, B_DIM // 4)
    bk_ref[...] = _rope(proj(OFF_KV + KV_BK), cb, sb, B_DIM // 4) * (B_DIM ** -0.5)
    bv_ref[...] = proj(OFF_KV + KV_BV).astype(BF16)
    bg_ref[...] = proj(OFF_BG)
    uc_ref[...] = proj(OFF_C, 2 * GROUP_W)
    ud_ref[...] = proj(OFF_D)


def _in_proj(x2, mod, g, w_bf, tables, seq, kv_rows, kv_offset, kv_bufs=None):
    t = x2.shape[0]
    b = t // seq
    tm = min(TOKEN_TILE, seq)
    per_seq = seq // tm
    assert kv_offset % tm == 0
    tok = lambda i: (i, 0)
    tab = pl.BlockSpec((tm, LANES), lambda i: (i % per_seq, 0))
    gw = lambda dt: jax.ShapeDtypeStruct((t, GROUP_W), dt)
    kv = jax.ShapeDtypeStruct((b, kv_rows, GROUP_W), BF16)
    gspec = pl.BlockSpec((tm, GROUP_W), tok)
    kvspec = pl.BlockSpec((1, tm, GROUP_W), lambda i: (i // per_seq, i % per_seq + kv_offset // tm, 0))
    in_specs = [
        pl.BlockSpec((tm, D_MODEL), tok),
        pl.BlockSpec((1, N_MOD, D_MODEL), lambda i: (i // per_seq, 0, 0)),
        pl.BlockSpec((1, D_MODEL), lambda i: (0, 0)),
        pl.BlockSpec((D_MODEL, D_IN), lambda i: (0, 0)),
        tab, tab, tab, tab,
    ]
    args = [x2, mod, g, w_bf, *tables]
    aliases = {}
    if kv_bufs is not None:
        aliases = {len(args): 1, len(args) + 1: 2}
        in_specs += [pl.BlockSpec(memory_space=pl.ANY)] * 2
        args += list(kv_bufs)
    return pl.pallas_call(
        _in_kernel,
        grid=(t // tm,),
        in_specs=in_specs,
        out_specs=[gspec, kvspec, kvspec] + [gspec] * 4 + [pl.BlockSpec((tm, 2 * GROUP_W), tok), gspec],
        out_shape=[gw(BF16), kv, kv, gw(F32), gw(F32), gw(BF16), gw(F32),
                   jax.ShapeDtypeStruct((t, 2 * GROUP_W), F32), gw(F32)],
        input_output_aliases=aliases,
        compiler_params=_cparams("arbitrary"),
        name="in_proj",
    )(*args)


def _attn_kernel(lambda_init, q_ref, k_ref, v_ref, lam_ref, g_ref, o_ref,
                 qs_scr, s_a, s_b, mx_a, mx_b, l_scr, acc_scr):
    nq = q_ref.shape[1]
    tq = min(ATTN_Q_TILE, nq)
    n_qt = nq // tq
    n_maps = 2 * A_HEADS
    kt = ATTN_KEY_TILE
    n_tiles = k_ref.shape[1] // kt
    per_iter = ATTN_TILES_PER_ITER if n_tiles % ATTN_TILES_PER_ITER == 0 else 1
    n_iter = n_tiles // per_iter
    lp = lam_ref[...]
    lam = (jnp.exp(jnp.sum(lp[0:1] * lp[1:2], axis=-1, keepdims=True))
           - jnp.exp(jnp.sum(lp[2:3] * lp[3:4], axis=-1, keepdims=True)) + lambda_init)
    lane = lax.broadcasted_iota(jnp.int32, (1, GROUP_W), 1)

    def q_rows(t):
        return pl.ds(pl.multiple_of(t * tq, tq), tq)

    def stack_queries(t):
        q = q_ref[0, q_rows(t), :]
        for u in range(n_maps):
            keep = (lane >= u * A_QKDIM) & (lane < (u + 1) * A_QKDIM)
            qs_scr[u * tq:(u + 1) * tq, :] = jnp.where(keep, q, jnp.zeros_like(q))

    def scores(j, s_scr, mx_scr):
        for t in range(per_iter):
            tile = j * per_iter + t
            r = pl.multiple_of(tile * kt, kt)
            s = _dot_nt(qs_scr[...], k_ref[0, pl.ds(r, kt), :])
            s_scr[tile] = s
            mx_scr[...] = jnp.maximum(mx_scr[...], jnp.maximum(s[:, :LANES], s[:, LANES:]))

    def values(j, s_scr, mx_scr):
        for t in range(per_iter):
            tile = j * per_iter + t
            r = pl.multiple_of(tile * kt, kt)
            m = mx_scr[...]
            e = jnp.exp2(s_scr[tile] - jnp.concatenate([m, m], axis=1))
            l_scr[...] = l_scr[...] + (e[:, :LANES] + e[:, LANES:])
            acc_scr[...] = acc_scr[...] + _dot(e.astype(BF16), v_ref[0, pl.ds(r, kt), :])

    def start_scores(t, mx_scr):
        stack_queries(t)
        mx_scr[...] = jnp.full(mx_scr.shape, -jnp.inf, F32)

    def end_scores(mx_scr):
        mx_scr[...] = jnp.broadcast_to(jnp.max(mx_scr[...], axis=-1, keepdims=True), mx_scr.shape)

    def start_values():
        l_scr[...] = jnp.zeros(l_scr.shape, F32)
        acc_scr[...] = jnp.zeros(acc_scr.shape, F32)

    def end_values(t):
        o = jnp.zeros((tq, GROUP_W), F32)
        for hd in range(A_HEADS):
            parts = []
            for m in range(2):
                u = 2 * hd + m
                rows = slice(u * tq, (u + 1) * tq)
                parts.append(acc_scr[rows, :] * (1.0 / jnp.sum(l_scr[rows, :], axis=-1, keepdims=True)))
            o = jnp.where((lane >= hd * A_VDIM) & (lane < (hd + 1) * A_VDIM), parts[0] - lam * parts[1], o)
        y = o * lax.rsqrt(_head_mean(o * o, A_VDIM) + EPS) * g_ref[...] * (1.0 - lambda_init)
        o_ref[0, q_rows(t), :] = y.astype(BF16)

    def overlapped(t, s_cur, mx_cur, s_nxt, mx_nxt):
        start_scores(t + 1, mx_nxt)
        start_values()

        def body(j, carry):
            scores(j, s_nxt, mx_nxt)
            values(j, s_cur, mx_cur)
            return carry

        lax.fori_loop(0, n_iter, body, 0)
        end_scores(mx_nxt)
        end_values(t)

    assert n_qt == 1 or n_qt % 2 == 0
    start_scores(0, mx_a)
    lax.fori_loop(0, n_iter, lambda j, c: (scores(j, s_a, mx_a), c)[1], 0)
    end_scores(mx_a)
    s_last, mx_last = s_a, mx_a
    if n_qt > 1:
        def pair(p, carry):
            overlapped(2 * p, s_a, mx_a, s_b, mx_b)
            overlapped(2 * p + 1, s_b, mx_b, s_a, mx_a)
            return carry

        lax.fori_loop(0, (n_qt - 2) // 2, pair, 0)
        overlapped(n_qt - 2, s_a, mx_a, s_b, mx_b)
        s_last, mx_last = s_b, mx_b
    start_values()
    lax.fori_loop(0, n_iter, lambda j, c: (values(j, s_last, mx_last), c)[1], 0)
    end_values(n_qt - 1)


def _attention(q, k, v, lam_p, subln_g, lambda_init, key_rows=None):
    b, nq, _ = q.shape
    k_off, nk = (0, k.shape[1]) if key_rows is None else key_rows
    assert k_off % nk == 0
    rows = 2 * A_HEADS * min(ATTN_Q_TILE, nq)
    seq = lambda n, blk=0: pl.BlockSpec((1, n, GROUP_W), lambda bi: (bi, blk, 0))
    s_shape = pltpu.VMEM((nk // ATTN_KEY_TILE, rows, ATTN_KEY_TILE), F32)
    return pl.pallas_call(
        functools.partial(_attn_kernel, lambda_init),
        grid=(b,),
        in_specs=[seq(nq), seq(nk, k_off // nk), seq(nk, k_off // nk),
                  pl.BlockSpec((4, A_QKDIM), lambda bi: (0, 0)),
                  pl.BlockSpec((1, GROUP_W), lambda bi: (0, 0))],
        out_specs=seq(nq),
        out_shape=jax.ShapeDtypeStruct((b, nq, GROUP_W), BF16),
        scratch_shapes=[pltpu.VMEM((rows, GROUP_W), BF16), s_shape, s_shape,
                        pltpu.VMEM((rows, LANES), F32), pltpu.VMEM((rows, LANES), F32),
                        pltpu.VMEM((rows, LANES), F32),
                        pltpu.VMEM((rows, GROUP_W), F32)],
        compiler_params=_cparams("arbitrary"),
        name="diff_attention",
    )(q, k, v, lam_p, jnp.tile(subln_g, A_HEADS).reshape(1, GROUP_W))


def _ret_kernel(q_ref, k_ref, v_ref, gate_ref, dec_ref, s0_ref, y_ref, st_ref, o_scr, s_scr, intra_scr, dec_scr):
    n = q_ref.shape[1]
    c = RET_CHUNK
    nch = n // c
    lane = lax.broadcasted_iota(jnp.int32, (1, GROUP_W), 1)
    head_masks = [(lane >= h * B_DIM) & (lane < (h + 1) * B_DIM) for h in range(B_HEADS)]
    rr = lax.broadcasted_iota(jnp.int32, (GROUP_W, GROUP_W), 0) // B_DIM
    cc = lax.broadcasted_iota(jnp.int32, (GROUP_W, GROUP_W), 1) // B_DIM
    block_diag = rr == cc
    pos = lax.broadcasted_iota(jnp.int32, (c, 1), 0).astype(F32)
    ii = lax.broadcasted_iota(jnp.int32, (c, c), 0)
    jj = lax.broadcasted_iota(jnp.int32, (c, c), 1)

    c_decs = []
    for d in range(2):
        lg = -jnp.exp(dec_ref[d:d + 1, :])
        rel = (ii - jj) if d == 0 else (jj - ii)
        relf = jnp.maximum(rel, 0).astype(F32)
        for h in range(B_HEADS):
            intra_scr[d, h * c:(h + 1) * c, :] = jnp.where(
                rel >= 0, jnp.exp(relf * lg[:, h * B_DIM:h * B_DIM + 1]), 0.0)
        if d == 0:
            dec_scr[d, 0] = jnp.exp((pos + 1.0) * lg)
            dec_scr[d, 1] = jnp.exp((c - 1.0 - pos) * lg)
        else:
            dec_scr[d, 0] = jnp.exp((c - pos) * lg)
            dec_scr[d, 1] = jnp.exp(pos * lg)
        c_decs.append(jnp.exp(float(c) * lg))
        s_scr[d] = s0_ref[0, d]

    def chunk(i, carry):
        for d in range(2):
            ci = i if d == 0 else nch - 1 - i
            r = pl.multiple_of(ci * c, c)
            qc = q_ref[0, pl.ds(r, c), :]
            kc = k_ref[0, pl.ds(r, c), :]
            vc = v_ref[0, pl.ds(r, c), :]
            qb = qc.astype(BF16)
            q4 = jnp.concatenate([jnp.where(mk, qb, jnp.zeros_like(qb)) for mk in head_masks], axis=0)
            sc = (_dot_nt(q4, kc.astype(BF16)) * intra_scr[d]).astype(BF16)
            state = s_scr[d]
            out = _dot((qc * dec_scr[d, 0]).astype(BF16), state.astype(BF16))
            for h in range(B_HEADS):
                out = out + jnp.where(head_masks[h], _dot(sc[h * c:(h + 1) * c], vc), 0.0)
            o_scr[d, pl.ds(r, c), :] = out
            kd_t = (kc * dec_scr[d, 1]).T.astype(BF16)
            s_scr[d] = c_decs[d] * state + jnp.where(block_diag, _dot(kd_t, vc), 0.0)
        return carry

    lax.fori_loop(0, nch, chunk, 0)
    st_ref[0] = s_scr[...]

    def finish(i, carry):
        r = pl.multiple_of(i * c, c)
        o = o_scr[0, pl.ds(r, c), :] + o_scr[1, pl.ds(r, c), :]
        y = _silu(gate_ref[0, pl.ds(r, c), :]) * (o * lax.rsqrt(_head_mean(o * o, B_DIM) + EPS))
        y_ref[0, pl.ds(r, c), :] = y.astype(BF16)
        return carry

    lax.fori_loop(0, nch, finish, 0)


def _retention(q, k, v, gate, dec_lanes, s0):
    b, n, _ = q.shape
    seq = pl.BlockSpec((1, n, GROUP_W), lambda bi: (bi, 0, 0))
    st = pl.BlockSpec((1, 2, GROUP_W, GROUP_W), lambda bi: (bi, 0, 0, 0))
    return pl.pallas_call(
        _ret_kernel,
        grid=(b,),
        in_specs=[seq, seq, seq, seq, pl.BlockSpec((2, GROUP_W), lambda bi: (0, 0)), st],
        out_specs=[seq, st],
        out_shape=[jax.ShapeDtypeStruct((b, n, GROUP_W), BF16),
                   jax.ShapeDtypeStruct((b, 2, GROUP_W, GROUP_W), F32)],
        scratch_shapes=[pltpu.VMEM((2, n, GROUP_W), F32), pltpu.VMEM((2, GROUP_W, GROUP_W), F32),
                        pltpu.VMEM((2, B_HEADS * RET_CHUNK, RET_CHUNK), F32),
                        pltpu.VMEM((2, 2, RET_CHUNK, GROUP_W), F32)],
        compiler_params=_cparams("arbitrary"),
        name="retention",
    )(q, k, v, gate, dec_lanes, s0)


def _convpool_kernel(uc_ref, ud_ref, cw_ref, cb_ref, lg_ref, lb_ref, pw_ref, ps_ref, yc_ref, yd_ref,
                     ubuf, dbuf, shift_scr):
    n = uc_ref.shape[1]
    rb = CONV_ROW_BLOCK
    for buf in (ubuf, dbuf):
        buf[0:PAD_ROWS, :] = jnp.zeros((PAD_ROWS, GROUP_W), F32)
        buf[PAD_ROWS + n:, :] = jnp.zeros((TAIL_ROWS, GROUP_W), F32)

    def fill(i, carry):
        r = pl.multiple_of(i * RET_CHUNK, RET_CHUNK)
        u2 = uc_ref[0, pl.ds(r, RET_CHUNK), :]
        ubuf[pl.ds(PAD_ROWS + r, RET_CHUNK), :] = u2[:, :GROUP_W] * _sigmoid(u2[:, GROUP_W:])
        dbuf[pl.ds(PAD_ROWS + r, RET_CHUNK), :] = ud_ref[0, pl.ds(r, RET_CHUNK), :]
        return carry

    lax.fori_loop(0, n // RET_CHUNK, fill, 0)

    lane = lax.broadcasted_iota(jnp.int32, (1, GROUP_W), 1)
    half = jnp.left_shift(1, lane // POOL_GROUP)
    first_tap = PAD_ROWS - C_KSIZE // 2

    def block(i, carry):
        r = pl.multiple_of(i * rb, rb)
        uwin = ubuf[pl.ds(r, rb + 2 * PAD_ROWS), :]
        dwin = dbuf[pl.ds(r, rb + PAD_ROWS + TAIL_ROWS), :]
        acc = jnp.zeros((rb, GROUP_W), F32) + cb_ref[...]
        for ph in range(SUBLANES):
            shift_scr[ph] = uwin[ph:ph + rb + 2 * PAD_ROWS - SUBLANES]
            for a8 in range(0, 2 * PAD_ROWS, SUBLANES):
                kk = a8 + ph - first_tap
                if 0 <= kk < C_KSIZE:
                    acc = acc + shift_scr[ph, a8:a8 + rb, :] * cw_ref[kk:kk + 1, :]
        mu = jnp.mean(acc, axis=-1, keepdims=True)
        var = jnp.mean(jnp.square(acc - mu), axis=-1, keepdims=True)
        yn = (acc - mu) * lax.rsqrt(var + EPS) * lg_ref[...] + lb_ref[...]
        yc_ref[0, pl.ds(r, rb), :] = _silu(yn).astype(BF16)
        d0 = dwin[:, :LANES]
        d1 = dwin[:, LANES:]
        p = PAD_ROWS
        s2 = d0[p - 1:p - 1 + rb] + d0[p:p + rb]
        s4 = s2 + d0[p - 2:p - 2 + rb] + d0[p + 1:p + 1 + rb]
        q2 = d1[0:rb + 32] + d1[1:rb + 33]
        q4 = q2[0:rb + 24] + q2[2:rb + 26]
        q8 = q4[0:rb + 16] + q4[4:rb + 20]
        s8 = q8[p - 4:p - 4 + rb]
        s16 = q8[p - 8:p - 8 + rb] + q8[p:p + rb]
        lane1 = lane[:, :LANES]
        tot = jnp.concatenate([jnp.where(lane1 < POOL_GROUP, s2, s4), jnp.where(lane1 < POOL_GROUP, s8, s16)],
                              axis=1)
        t = r + lax.broadcasted_iota(jnp.int32, (rb, 1), 0)
        cnt = (jnp.minimum(t + half, n) - jnp.maximum(t - half, 0)).astype(F32)
        pooled = tot / cnt - dwin[p:p + rb]
        yd_ref[0, pl.ds(r, rb), :] = (_dot(pooled.astype(BF16), pw_ref[...]) * ps_ref[...]).astype(BF16)
        return carry

    lax.fori_loop(0, n // rb, block, 0)


def _convpool(uc, ud, cw, cb, lng, lnb, pw_bd, pscale):
    b, n, _ = ud.shape
    seq = lambda w: pl.BlockSpec((1, n, w), lambda bi: (bi, 0, 0))
    row = pl.BlockSpec((1, GROUP_W), lambda bi: (0, 0))
    return pl.pallas_call(
        _convpool_kernel,
        grid=(b,),
        in_specs=[seq(2 * GROUP_W), seq(GROUP_W),
                  pl.BlockSpec((C_KSIZE, GROUP_W), lambda bi: (0, 0)), row, row, row,
                  pl.BlockSpec((GROUP_W, GROUP_W), lambda bi: (0, 0)), row],
        out_specs=[seq(GROUP_W), seq(GROUP_W)],
        out_shape=[jax.ShapeDtypeStruct((b, n, GROUP_W), BF16)] * 2,
        scratch_shapes=[pltpu.VMEM((n + PAD_ROWS + TAIL_ROWS, GROUP_W), F32)] * 2
                       + [pltpu.VMEM((SUBLANES, CONV_ROW_BLOCK + 2 * PAD_ROWS - SUBLANES, GROUP_W), F32)],
        compiler_params=_cparams("arbitrary"),
        name="conv_pool",
    )(uc, ud, cw, cb, lng, lnb, pw_bd, pscale)


def _tail_kernel(per_seq, final, *refs):
    y_refs = refs[0:12]
    x_ref, xp_ref, xn_ref = refs[12:15]
    wo_ref, wu_ref, dw_ref, db_ref, wd_ref, mod_ref, g_ref, fg_ref, o_ref = refs[15:24]
    ybuf, xbuf, hbuf, ubuf_a, ubuf_b, acc = refs[24:]
    tm = x_ref.shape[0]
    hr = BF16_ROWS
    i = pl.program_id(0)
    first = (i % per_seq) == 0
    last = (i % per_seq) == per_seq - 1

    for gi in range(N_MIXERS):
        y_ref, yp_ref, yn_ref = y_refs[3 * gi:3 * gi + 3]
        ybuf[gi, 0:hr, :] = yp_ref[...]
        ybuf[gi, hr:hr + tm, :] = y_ref[...]
        ybuf[gi, hr + tm:, :] = yn_ref[...]
    xbuf[0:hr, :] = xp_ref[...]
    xbuf[hr:hr + tm, :] = x_ref[...]
    xbuf[hr + tm:, :] = xn_ref[...]
    y = jnp.zeros(xbuf.shape, F32)
    for gi in range(N_MIXERS):
        y = y + _dot(ybuf[gi], wo_ref[gi * GROUP_W:(gi + 1) * GROUP_W, :])
    xbuf[...] = xbuf[...] + mod_ref[0, 2:3, :] * y
    h = _norm_mod(xbuf[...], g_ref[...], mod_ref[0, 3:4, :], mod_ref[0, 4:5, :]).astype(BF16)
    row = lax.broadcasted_iota(jnp.int32, (tm + 2 * hr, 1), 0)
    outside = (first & (row < hr)) | (last & (row >= hr + tm))
    hbuf[...] = jnp.where(outside, jnp.zeros_like(h), h)

    acc[...] = jnp.zeros_like(acc)
    tf = FFN_CHUNK
    nj = wu_ref.shape[0]

    def up(j, buf):
        buf[...] = _dot(hbuf[...], wu_ref[j])

    def mix(j, buf):
        w = dw_ref[j]
        u = db_ref[j]
        for kk in range(FFN_KSIZE):
            u = u + buf[pl.ds(hr - FFN_KSIZE // 2 + kk, tm), :] * w[kk:kk + 1, :]
        act = (u[:, :tf] * _silu(u[:, tf:])).astype(BF16)
        acc[...] = acc[...] + _dot(act, wd_ref[j])

    up(0, ubuf_a)

    def pair(p, carry):
        up(2 * p + 1, ubuf_b)
        mix(2 * p, ubuf_a)
        up(2 * p + 2, ubuf_a)
        mix(2 * p + 1, ubuf_b)
        return carry

    lax.fori_loop(0, (nj - 1) // 2, pair, 0)
    mix(nj - 1, ubuf_a)
    x = xbuf[hr:hr + tm, :] + mod_ref[0, 5:6, :] * acc[...]
    if final:
        ms = jnp.mean(x * x, axis=-1, keepdims=True)
        x = x * lax.rsqrt(ms + EPS) * fg_ref[...]
    o_ref[...] = x


def _tail(ys, w_out_bf, wu_r, dw_r, db_r, wd_r, x2, mod, g2, final_g, seq, final):
    t = x2.shape[0]
    tm = min(TOKEN_TILE, seq)
    per_seq = seq // tm
    nblk = tm // BF16_ROWS
    nj = wu_r.shape[0]
    assert nj % 2 == 1
    tok = lambda i: (i, 0)
    prev = lambda i: (jnp.maximum(i * nblk - 1, 0), 0)
    nxt = lambda i: (jnp.minimum((i + 1) * nblk, t // BF16_ROWS - 1), 0)
    const = lambda shape: pl.BlockSpec(shape, lambda i: (0,) * len(shape), pipeline_mode=pl.Buffered(1))
    trio = lambda w: [pl.BlockSpec((tm, w), tok), pl.BlockSpec((BF16_ROWS, w), prev), pl.BlockSpec((BF16_ROWS, w), nxt)]
    halo_rows = tm + 2 * BF16_ROWS
    args = []
    for y in ys:
        args += [y, y, y]
    return pl.pallas_call(
        functools.partial(_tail_kernel, per_seq, final),
        grid=(t // tm,),
        in_specs=trio(GROUP_W) * N_MIXERS + trio(D_MODEL)
                 + [const((D_MODEL, D_MODEL)),
                    const((nj, D_MODEL, 2 * FFN_CHUNK)), const((nj, FFN_KSIZE, 2 * FFN_CHUNK)),
                    const((nj, 1, 2 * FFN_CHUNK)), const((nj, FFN_CHUNK, D_MODEL)),
                    pl.BlockSpec((1, N_MOD, D_MODEL), lambda i: (i // per_seq, 0, 0)),
                    const((1, D_MODEL)), const((1, D_MODEL))],
        out_specs=pl.BlockSpec((tm, D_MODEL), tok),
        out_shape=jax.ShapeDtypeStruct((t, D_MODEL), F32),
        scratch_shapes=[pltpu.VMEM((N_MIXERS, halo_rows, GROUP_W), BF16),
                        pltpu.VMEM((halo_rows, D_MODEL), F32),
                        pltpu.VMEM((halo_rows, D_MODEL), BF16),
                        pltpu.VMEM((halo_rows, 2 * FFN_CHUNK), F32),
                        pltpu.VMEM((halo_rows, 2 * FFN_CHUNK), F32),
                        pltpu.VMEM((tm, D_MODEL), F32)],
        compiler_params=_cparams("arbitrary"),
        name="out_ffn",
    )(*args, x2, x2, x2, w_out_bf, wu_r, dw_r, db_r, wd_r, mod, g2, final_g)


def _split_ffn(a, lead):
    nj = D_FF // FFN_CHUNK
    v = a[..., :D_FF].reshape(lead + (nj, FFN_CHUNK))
    g = a[..., D_FF:].reshape(lead + (nj, FFN_CHUNK))
    both = jnp.concatenate([v, g], axis=-1)
    return jnp.moveaxis(both, -2, 0)


def kernel(x, c, ctx, c_ctx, w_mod, b_mod, norm1_g, norm2_g, w_in, w_out, diff_lambda, diff_subln_g, ret_decay,
           conv_dw_w, conv_dw_b, conv_ln_g, conv_ln_b, pool_w, pool_scale, ffn_w_up, ffn_dw_w, ffn_dw_b,
           ffn_w_down, final_g):
    b, n, d = x.shape
    nc = ctx.shape[1]
    depth = w_mod.shape[0]
    assert d == D_MODEL and b + 1 <= MOD_ROWS
    assert n % RET_CHUNK == 0 and nc % RET_CHUNK == 0 and n % min(TOKEN_TILE, n) == 0

    cond = jnp.zeros((MOD_ROWS, D_MODEL), F32).at[:b].set(c).at[b].set(c_ctx)
    mods = _modulation(cond, w_mod, b_mod)

    tab_x = _rope_tables(n, A_QKDIM // 4) + _rope_tables(n, B_DIM // 4)
    one, zero = jnp.ones((nc, LANES), F32), jnp.zeros((nc, LANES), F32)
    tab_c = (one, zero, one, zero)

    x2 = x.reshape(b * n, d)
    c2 = ctx.reshape(b * nc, d)
    row = lambda v: v.reshape(1, -1)
    g3 = lambda a, s: a.reshape(b, s, GROUP_W)
    zero_state = jnp.zeros((b, 2, GROUP_W, GROUP_W), F32)

    for l in range(depth):
        need_ctx = l < depth - 1
        lambda_init = 0.8 - 0.6 * math.exp(-0.3 * l)
        mod_x = mods[l, :b].reshape(b, N_MOD, D_MODEL)
        mod_c = jnp.broadcast_to(mods[l, b].reshape(1, N_MOD, D_MODEL), (b, N_MOD, D_MODEL))
        w_in_bf = w_in[l].astype(BF16)
        w_out_bf = w_out[l].astype(BF16)
        wu_r = _split_ffn(ffn_w_up[l].astype(BF16), (D_MODEL,))
        dw_r = _split_ffn(ffn_dw_w[l], (FFN_KSIZE,))
        db_r = _split_ffn(ffn_dw_b[l].reshape(1, -1), (1,))
        wd_r = ffn_w_down[l].astype(BF16).reshape(D_FF // FFN_CHUNK, FFN_CHUNK, D_MODEL)
        dec_lanes = jnp.repeat(ret_decay[l], B_DIM, axis=-1)
        pw_bd = jnp.zeros((GROUP_W, GROUP_W), F32)
        for gi in range(len(POOL_WINDOWS)):
            pw_bd = pw_bd.at[gi * POOL_GROUP:(gi + 1) * POOL_GROUP, gi * POOL_GROUP:(gi + 1) * POOL_GROUP].set(
                pool_w[l, gi])
        pw_bd = pw_bd.astype(BF16)
        mixer_params = (conv_dw_w[l], row(conv_dw_b[l]), row(conv_ln_g[l]), row(conv_ln_b[l]), pw_bd,
                        row(pool_scale[l]))

        pc = _in_proj(c2, mod_c, row(norm1_g[l]), w_in_bf, tab_c, nc, n + nc, n)
        aq_c, ak_part, av_part, bq_c, bk_c, bv_c, bg_c, uc_c, ud_c = pc
        px = _in_proj(x2, mod_x, row(norm1_g[l]), w_in_bf, tab_x, n, n + nc, 0, kv_bufs=(ak_part, av_part))
        aq_x, ak_all, av_all, bq_x, bk_x, bv_x, bg_x, uc_x, ud_x = px

        ya_x = _attention(g3(aq_x, n), ak_all, av_all, diff_lambda[l], diff_subln_g[l], lambda_init)
        yb_c, st_c = _retention(g3(bq_c, nc), g3(bk_c, nc), g3(bv_c, nc), g3(bg_c, nc), dec_lanes, zero_state)
        yb_x, _ = _retention(g3(bq_x, n), g3(bk_x, n), g3(bv_x, n), g3(bg_x, n), dec_lanes, st_c)
        yc_x, yd_x = _convpool(uc_x.reshape(b, n, 2 * GROUP_W), g3(ud_x, n), *mixer_params)
        flat = lambda a: a.reshape(-1, GROUP_W)
        x2 = _tail([flat(ya_x), flat(yb_x), flat(yc_x), flat(yd_x)], w_out_bf, wu_r, dw_r, db_r, wd_r, x2, mod_x,
                   row(norm2_g[l]), row(final_g), n, final=not need_ctx)

        if need_ctx:
            ya_c = _attention(g3(aq_c, nc), ak_all, av_all, diff_lambda[l], diff_subln_g[l], lambda_init,
                              key_rows=(n, nc))
            yc_c, yd_c = _convpool(uc_c.reshape(b, nc, 2 * GROUP_W), g3(ud_c, nc), *mixer_params)
            c2 = _tail([flat(ya_c), flat(yb_c), flat(yc_c), flat(yd_c)], w_out_bf, wu_r, dw_r, db_r, wd_r, c2,
                       mod_c, row(norm2_g[l]), row(final_g), nc, final=False)

    return x2.reshape(b, n, d)
```

```python
import functools
import math

import numpy as np
import jax
import jax.numpy as jnp
from jax import lax
from jax.experimental import pallas as pl
from jax.experimental.pallas import tpu as pltpu

F32 = jnp.float32
BF16 = jnp.bfloat16

D_MODEL = 1024
DEPTH = 4
GRID_W = 64
N_MIXERS = 4
GROUP_W = D_MODEL // N_MIXERS
A_HEADS = 4
A_VDIM = GROUP_W // A_HEADS
A_QKDIM = A_VDIM // 2
B_HEADS = 4
B_DIM = GROUP_W // B_HEADS
C_KSIZE = 31
POOL_WINDOWS = (2, 4, 8, 16)
POOL_GROUP = GROUP_W // len(POOL_WINDOWS)
D_FF = 2816
FFN_KSIZE = 3
RET_CHUNK = 256
ROPE_BASE = 10000.0
EPS = 1e-6
N_MOD = 6
OFF_AQ = 0
OFF_BQ = OFF_AQ + GROUP_W
OFF_BG = OFF_BQ + GROUP_W
OFF_C = OFF_BG + GROUP_W
OFF_D = OFF_C + 2 * GROUP_W
OFF_KV = OFF_D + GROUP_W
KV_AK = 0
KV_AV = GROUP_W
KV_BK = 2 * GROUP_W
KV_BV = 3 * GROUP_W
D_IN = OFF_KV + 4 * GROUP_W

LANES = 128
BF16_ROWS = 16
MOD_ROWS = 16
TOKEN_TILE = 1024
ATTN_Q_TILE = 256
ATTN_KEY_TILE = 256
ATTN_TILES_PER_ITER = 3
LOG2E = 1.4426950408889634
CONV_ROW_BLOCK = 128
SUBLANES = 8
PAD_ROWS = 16
TAIL_ROWS = 32
RET_BATCHES = 2
FFN_CHUNK = 256
VMEM_LIMIT = 58 * 1024 * 1024


def _cparams(*sem):
    return pltpu.CompilerParams(dimension_semantics=sem, vmem_limit_bytes=VMEM_LIMIT)


def _sigmoid(x):
    return 1.0 / (1.0 + jnp.exp(-x))


def _silu(x):
    return x * _sigmoid(x)


def _dot(a, b):
    return jnp.dot(a, b, preferred_element_type=F32)


def _dot_nt(a, b):
    return lax.dot_general(a, b, (((1,), (1,)), ((), ())), preferred_element_type=F32)


def _rope_tables(n, blk):
    lane = np.arange(LANES)
    freq = lane % blk
    is_a = (lane % (2 * blk)) < blk
    is_row = (lane % (4 * blk)) < 2 * blk
    inv = ROPE_BASE ** (-freq.astype(np.float64) / blk)
    t = np.arange(n)
    pos = np.where(is_row[None, :], (t // GRID_W)[:, None], (t % GRID_W)[:, None]).astype(np.float64)
    ang = pos * inv[None, :]
    cos = np.cos(ang)
    sin = np.sin(ang) * np.where(is_a, -1.0, 1.0)[None, :]
    return jnp.asarray(cos, F32), jnp.asarray(sin, F32)


def _rope(x, cos, sin, blk):
    lane = lax.broadcasted_iota(jnp.int32, (1, LANES), 1)
    is_a = (lane % (2 * blk)) < blk
    outs = []
    for j in range(GROUP_W // LANES):
        xh = x[:, j * LANES:(j + 1) * LANES]
        from_hi = pltpu.roll(xh, LANES - blk, axis=1)
        from_lo = pltpu.roll(xh, blk, axis=1)
        outs.append(xh * cos + jnp.where(is_a, from_hi, from_lo) * sin)
    return jnp.concatenate(outs, axis=1)


def _head_mean(a, width):
    r = lax.broadcasted_iota(jnp.int32, (GROUP_W, GROUP_W), 0) // width
    c = lax.broadcasted_iota(jnp.int32, (GROUP_W, GROUP_W), 1) // width
    seg = jnp.where(r == c, 1.0 / width, 0.0).astype(BF16)
    a1 = a.astype(BF16)
    r1 = a - a1.astype(F32)
    a2 = r1.astype(BF16)
    a3 = (r1 - a2.astype(F32)).astype(BF16)
    return _dot(a1, seg) + _dot(a2, seg) + _dot(a3, seg)


def _mod_kernel(s_ref, w_ref, b_ref, o_ref):
    s = _silu(s_ref[...])
    o_ref[0] = _dot(s.astype(BF16), w_ref[0].astype(BF16)) + b_ref[0]


def _modulation(cond, w_mod, b_mod):
    depth = w_mod.shape[0]
    return pl.pallas_call(
        _mod_kernel,
        grid=(depth, N_MOD),
        in_specs=[
            pl.BlockSpec((MOD_ROWS, D_MODEL), lambda l, j: (0, 0)),
            pl.BlockSpec((1, D_MODEL, D_MODEL), lambda l, j: (l, 0, j)),
            pl.BlockSpec((1, 1, D_MODEL), lambda l, j: (l, 0, j)),
        ],
        out_specs=pl.BlockSpec((1, MOD_ROWS, D_MODEL), lambda l, j: (l, 0, j)),
        out_shape=jax.ShapeDtypeStruct((depth, MOD_ROWS, N_MOD * D_MODEL), F32),
        compiler_params=_cparams("arbitrary", "arbitrary"),
        name="modulation",
    )(cond, w_mod, b_mod.reshape(depth, 1, N_MOD * D_MODEL))


def _norm_mod(x, g, shift, scale):
    ms = jnp.mean(x * x, axis=-1, keepdims=True)
    return (x * lax.rsqrt(ms + EPS) * g) * (1.0 + scale) + shift


def _in_kernel(x_ref, mod_ref, g_ref, w_ref, ca_ref, sa_ref, cb_ref, sb_ref, *refs):
    aq_ref, ak_ref, av_ref, bq_ref, bk_ref, bv_ref, bg_ref, uc_ref, ud_ref = refs[-9:]
    h = _norm_mod(x_ref[...], g_ref[...], mod_ref[0, 0:1, :], mod_ref[0, 1:2, :]).astype(BF16)

    def proj(off, width=GROUP_W):
        return _dot(h, w_ref[:, off:off + width])

    ca, sa, cb, sb = ca_ref[...], sa_ref[...], cb_ref[...], sb_ref[...]
    aq_ref[...] = (_rope(proj(OFF_AQ), ca, sa, A_QKDIM // 4) * (A_QKDIM ** -0.5 * LOG2E)).astype(BF16)
    ak_ref[0] = _rope(proj(OFF_KV + KV_AK), ca, sa, A_QKDIM // 4).astype(BF16)
    av_ref[0] = proj(OFF_KV + KV_AV).astype(BF16)
    bq_ref[...] = _rope(proj(OFF_BQ), cb, sb, B_DIM // 4)
    bk_ref[...] = _rope(proj(OFF_KV + KV_BK), cb, sb, B_DIM // 4) * (B_DIM ** -0.5)
    bv_ref[...] = proj(OFF_KV + KV_BV).astype(BF16)
    bg_ref[...] = proj(OFF_BG)
    uc_ref[...] = proj(OFF_C, 2 * GROUP_W)
    ud_ref[...] = proj(OFF_D)


def _in_proj(x2, mod, g, w_bf, tables, seq, kv_rows, kv_offset, kv_bufs=None):
    t = x2.shape[0]
    b = t // seq
    tm = min(TOKEN_TILE, seq)
    per_seq = seq // tm
    assert kv_offset % tm == 0
    tok = lambda i: (i, 0)
    tab = pl.BlockSpec((tm, LANES), lambda i: (i % per_seq, 0))
    gw = lambda dt: jax.ShapeDtypeStruct((t, GROUP_W), dt)
    kv = jax.ShapeDtypeStruct((b, kv_rows, GROUP_W), BF16)
    gspec = pl.BlockSpec((tm, GROUP_W), tok)
    kvspec = pl.BlockSpec((1, tm, GROUP_W), lambda i: (i // per_seq, i % per_seq + kv_offset // tm, 0))
    in_specs = [
        pl.BlockSpec((tm, D_MODEL), tok),
        pl.BlockSpec((1, N_MOD, D_MODEL), lambda i: (i // per_seq, 0, 0)),
        pl.BlockSpec((1, D_MODEL), lambda i: (0, 0)),
        pl.BlockSpec((D_MODEL, D_IN), lambda i: (0, 0)),
        tab, tab, tab, tab,
    ]
    args = [x2, mod, g, w_bf, *tables]
    aliases = {}
    if kv_bufs is not None:
        aliases = {len(args): 1, len(args) + 1: 2}
        in_specs += [pl.BlockSpec(memory_space=pl.ANY)] * 2
        args += list(kv_bufs)
    return pl.pallas_call(
        _in_kernel,
        grid=(t // tm,),
        in_specs=in_specs,
        out_specs=[gspec, kvspec, kvspec] + [gspec] * 4 + [pl.BlockSpec((tm, 2 * GROUP_W), tok), gspec],
        out_shape=[gw(BF16), kv, kv, gw(F32), gw(F32), gw(BF16), gw(F32),
                   jax.ShapeDtypeStruct((t, 2 * GROUP_W), F32), gw(F32)],
        input_output_aliases=aliases,
        compiler_params=_cparams("arbitrary"),
        name="in_proj",
    )(*args)


def _attn_kernel(lambda_init, q_ref, k_ref, v_ref, lam_ref, g_ref, o_ref,
                 qs_scr, s_a, s_b, mx_a, mx_b, l_scr, acc_scr):
    nq = q_ref.shape[1]
    tq = min(ATTN_Q_TILE, nq)
    n_qt = nq // tq
    n_maps = 2 * A_HEADS
    kt = ATTN_KEY_TILE
    n_tiles = k_ref.shape[1] // kt
    per_iter = ATTN_TILES_PER_ITER if n_tiles % ATTN_TILES_PER_ITER == 0 else 1
    n_iter = n_tiles // per_iter
    lp = lam_ref[...]
    lam = (jnp.exp(jnp.sum(lp[0:1] * lp[1:2], axis=-1, keepdims=True))
           - jnp.exp(jnp.sum(lp[2:3] * lp[3:4], axis=-1, keepdims=True)) + lambda_init)
    lane = lax.broadcasted_iota(jnp.int32, (1, GROUP_W), 1)

    def q_rows(t):
        return pl.ds(pl.multiple_of(t * tq, tq), tq)

    def stack_queries(t):
        q = q_ref[0, q_rows(t), :]
        for u in range(n_maps):
            keep = (lane >= u * A_QKDIM) & (lane < (u + 1) * A_QKDIM)
            qs_scr[u * tq:(u + 1) * tq, :] = jnp.where(keep, q, jnp.zeros_like(q))

    def scores(j, s_scr, mx_scr):
        for t in range(per_iter):
            tile = j * per_iter + t
            r = pl.multiple_of(tile * kt, kt)
            s = _dot_nt(qs_scr[...], k_ref[0, pl.ds(r, kt), :])
            s_scr[tile] = s
            mx_scr[...] = jnp.maximum(mx_scr[...], jnp.maximum(s[:, :LANES], s[:, LANES:]))

    def values(j, s_scr, mx_scr):
        for t in range(per_iter):
            tile = j * per_iter + t
            r = pl.multiple_of(tile * kt, kt)
            m = mx_scr[...]
            e = jnp.exp2(s_scr[tile] - jnp.concatenate([m, m], axis=1))
            l_scr[...] = l_scr[...] + (e[:, :LANES] + e[:, LANES:])
            acc_scr[...] = acc_scr[...] + _dot(e.astype(BF16), v_ref[0, pl.ds(r, kt), :])

    def start_scores(t, mx_scr):
        stack_queries(t)
        mx_scr[...] = jnp.full(mx_scr.shape, -jnp.inf, F32)

    def end_scores(mx_scr):
        mx_scr[...] = jnp.broadcast_to(jnp.max(mx_scr[...], axis=-1, keepdims=True), mx_scr.shape)

    def start_values():
        l_scr[...] = jnp.zeros(l_scr.shape, F32)
        acc_scr[...] = jnp.zeros(acc_scr.shape, F32)

    def end_values(t):
        o = jnp.zeros((tq, GROUP_W), F32)
        for hd in range(A_HEADS):
            parts = []
            for m in range(2):
                u = 2 * hd + m
                rows = slice(u * tq, (u + 1) * tq)
                parts.append(acc_scr[rows, :] * (1.0 / jnp.sum(l_scr[rows, :], axis=-1, keepdims=True)))
            o = jnp.where((lane >= hd * A_VDIM) & (lane < (hd + 1) * A_VDIM), parts[0] - lam * parts[1], o)
        y = o * lax.rsqrt(_head_mean(o * o, A_VDIM) + EPS) * g_ref[...] * (1.0 - lambda_init)
        o_ref[0, q_rows(t), :] = y.astype(BF16)

    def overlapped(t, s_cur, mx_cur, s_nxt, mx_nxt):
        start_scores(t + 1, mx_nxt)
        start_values()

        def body(j, carry):
            scores(j, s_nxt, mx_nxt)
            values(j, s_cur, mx_cur)
            return carry

        lax.fori_loop(0, n_iter, body, 0)
        end_scores(mx_nxt)
        end_values(t)

    assert n_qt == 1 or n_qt % 2 == 0
    start_scores(0, mx_a)
    lax.fori_loop(0, n_iter, lambda j, c: (scores(j, s_a, mx_a), c)[1], 0)
    end_scores(mx_a)
    s_last, mx_last = s_a, mx_a
    if n_qt > 1:
        def pair(p, carry):
            overlapped(2 * p, s_a, mx_a, s_b, mx_b)
            overlapped(2 * p + 1, s_b, mx_b, s_a, mx_a)
            return carry

        lax.fori_loop(0, (n_qt - 2) // 2, pair, 0)
        overlapped(n_qt - 2, s_a, mx_a, s_b, mx_b)
        s_last, mx_last = s_b, mx_b
    start_values()
    lax.fori_loop(0, n_iter, lambda j, c: (values(j, s_last, mx_last), c)[1], 0)
    end_values(n_qt - 1)


def _attention(q, k, v, lam_p, subln_g, lambda_init, key_rows=None):
    b, nq, _ = q.shape
    k_off, nk = (0, k.shape[1]) if key_rows is None else key_rows
    assert k_off % nk == 0
    rows = 2 * A_HEADS * min(ATTN_Q_TILE, nq)
    seq = lambda n, blk=0: pl.BlockSpec((1, n, GROUP_W), lambda bi: (bi, blk, 0))
    s_shape = pltpu.VMEM((nk // ATTN_KEY_TILE, rows, ATTN_KEY_TILE), F32)
    return pl.pallas_call(
        functools.partial(_attn_kernel, lambda_init),
        grid=(b,),
        in_specs=[seq(nq), seq(nk, k_off // nk), seq(nk, k_off // nk),
                  pl.BlockSpec((4, A_QKDIM), lambda bi: (0, 0)),
                  pl.BlockSpec((1, GROUP_W), lambda bi: (0, 0))],
        out_specs=seq(nq),
        out_shape=jax.ShapeDtypeStruct((b, nq, GROUP_W), BF16),
        scratch_shapes=[pltpu.VMEM((rows, GROUP_W), BF16), s_shape, s_shape,
                        pltpu.VMEM((rows, LANES), F32), pltpu.VMEM((rows, LANES), F32),
                        pltpu.VMEM((rows, LANES), F32),
                        pltpu.VMEM((rows, GROUP_W), F32)],
        compiler_params=_cparams("arbitrary"),
        name="diff_attention",
    )(q, k, v, lam_p, jnp.tile(subln_g, A_HEADS).reshape(1, GROUP_W))


def _ret_kernel(q_ref, k_ref, v_ref, gate_ref, dec_ref, s0_ref, y_ref, st_ref, o_scr, s_scr, intra_scr, dec_scr):
    nb, n = q_ref.shape[0], q_ref.shape[1]
    c = RET_CHUNK
    nch = n // c
    lane = lax.broadcasted_iota(jnp.int32, (1, GROUP_W), 1)
    head_masks = [(lane >= h * B_DIM) & (lane < (h + 1) * B_DIM) for h in range(B_HEADS)]
    rr = lax.broadcasted_iota(jnp.int32, (GROUP_W, GROUP_W), 0) // B_DIM
    cc = lax.broadcasted_iota(jnp.int32, (GROUP_W, GROUP_W), 1) // B_DIM
    block_diag = rr == cc
    pos = lax.broadcasted_iota(jnp.int32, (c, 1), 0).astype(F32)
    ii = lax.broadcasted_iota(jnp.int32, (c, c), 0)
    jj = lax.broadcasted_iota(jnp.int32, (c, c), 1)

    c_decs = []
    for d in range(2):
        lg = -jnp.exp(dec_ref[d:d + 1, :])
        rel = (ii - jj) if d == 0 else (jj - ii)
        relf = jnp.maximum(rel, 0).astype(F32)
        for h in range(B_HEADS):
            intra_scr[d, h * c:(h + 1) * c, :] = jnp.where(
                rel >= 0, jnp.exp(relf * lg[:, h * B_DIM:h * B_DIM + 1]), 0.0)
        if d == 0:
            dec_scr[d, 0] = jnp.exp((pos + 1.0) * lg)
            dec_scr[d, 1] = jnp.exp((c - 1.0 - pos) * lg)
        else:
            dec_scr[d, 0] = jnp.exp((c - pos) * lg)
            dec_scr[d, 1] = jnp.exp(pos * lg)
        c_decs.append(jnp.exp(float(c) * lg))
    s_scr[...] = s0_ref[...]

    def chunk(i, carry):
        for bi in range(nb):
            for d in range(2):
                ci = i if d == 0 else nch - 1 - i
                r = pl.multiple_of(ci * c, c)
                qc = q_ref[bi, pl.ds(r, c), :]
                kc = k_ref[bi, pl.ds(r, c), :]
                vc = v_ref[bi, pl.ds(r, c), :]
                qb = qc.astype(BF16)
                q4 = jnp.concatenate([jnp.where(mk, qb, jnp.zeros_like(qb)) for mk in head_masks], axis=0)
                sc = (_dot_nt(q4, kc.astype(BF16)) * intra_scr[d]).astype(BF16)
                state = s_scr[bi, d]
                out = _dot((qc * dec_scr[d, 0]).astype(BF16), state.astype(BF16))
                for h in range(B_HEADS):
                    out = out + jnp.where(head_masks[h], _dot(sc[h * c:(h + 1) * c], vc), 0.0)
                o_scr[bi, d, pl.ds(r, c), :] = out
                kd_t = (kc * dec_scr[d, 1]).T.astype(BF16)
                s_scr[bi, d] = c_decs[d] * state + jnp.where(block_diag, _dot(kd_t, vc), 0.0)
        return carry

    lax.fori_loop(0, nch, chunk, 0)
    st_ref[...] = s_scr[...]

    def finish(i, carry):
        r = pl.multiple_of(i * c, c)
        for bi in range(nb):
            o = o_scr[bi, 0, pl.ds(r, c), :] + o_scr[bi, 1, pl.ds(r, c), :]
            y = _silu(gate_ref[bi, pl.ds(r, c), :]) * (o * lax.rsqrt(_head_mean(o * o, B_DIM) + EPS))
            y_ref[bi, pl.ds(r, c), :] = y.astype(BF16)
        return carry

    lax.fori_loop(0, nch, finish, 0)


def _retention(q, k, v, gate, dec_lanes, s0):
    b, n, _ = q.shape
    nb = RET_BATCHES if b % RET_BATCHES == 0 else 1
    seq = pl.BlockSpec((nb, n, GROUP_W), lambda bi: (bi, 0, 0))
    st = pl.BlockSpec((nb, 2, GROUP_W, GROUP_W), lambda bi: (bi, 0, 0, 0))
    return pl.pallas_call(
        _ret_kernel,
        grid=(b // nb,),
        in_specs=[seq, seq, seq, seq, pl.BlockSpec((2, GROUP_W), lambda bi: (0, 0)), st],
        out_specs=[seq, st],
        out_shape=[jax.ShapeDtypeStruct((b, n, GROUP_W), BF16),
                   jax.ShapeDtypeStruct((b, 2, GROUP_W, GROUP_W), F32)],
        scratch_shapes=[pltpu.VMEM((nb, 2, n, GROUP_W), F32), pltpu.VMEM((nb, 2, GROUP_W, GROUP_W), F32),
                        pltpu.VMEM((2, B_HEADS * RET_CHUNK, RET_CHUNK), F32),
                        pltpu.VMEM((2, 2, RET_CHUNK, GROUP_W), F32)],
        compiler_params=_cparams("arbitrary"),
        name="retention",
    )(q, k, v, gate, dec_lanes, s0)


def _convpool_kernel(uc_ref, ud_ref, cw_ref, cb_ref, lg_ref, lb_ref, pw_ref, ps_ref, yc_ref, yd_ref,
                     ubuf, dbuf, shift_scr):
    n = uc_ref.shape[1]
    rb = CONV_ROW_BLOCK
    for buf in (ubuf, dbuf):
        buf[0:PAD_ROWS, :] = jnp.zeros((PAD_ROWS, GROUP_W), F32)
        buf[PAD_ROWS + n:, :] = jnp.zeros((TAIL_ROWS, GROUP_W), F32)

    def fill(i, carry):
        r = pl.multiple_of(i * RET_CHUNK, RET_CHUNK)
        u2 = uc_ref[0, pl.ds(r, RET_CHUNK), :]
        ubuf[pl.ds(PAD_ROWS + r, RET_CHUNK), :] = u2[:, :GROUP_W] * _sigmoid(u2[:, GROUP_W:])
        dbuf[pl.ds(PAD_ROWS + r, RET_CHUNK), :] = ud_ref[0, pl.ds(r, RET_CHUNK), :]
        return carry

    lax.fori_loop(0, n // RET_CHUNK, fill, 0)

    lane = lax.broadcasted_iota(jnp.int32, (1, GROUP_W), 1)
    half = jnp.left_shift(1, lane // POOL_GROUP)
    first_tap = PAD_ROWS - C_KSIZE // 2

    def block(i, carry):
        r = pl.multiple_of(i * rb, rb)
        uwin = ubuf[pl.ds(r, rb + 2 * PAD_ROWS), :]
        dwin = dbuf[pl.ds(r, rb + PAD_ROWS + TAIL_ROWS), :]
        acc = jnp.zeros((rb, GROUP_W), F32) + cb_ref[...]
        for ph in range(SUBLANES):
            shift_scr[ph] = uwin[ph:ph + rb + 2 * PAD_ROWS - SUBLANES]
            for a8 in range(0, 2 * PAD_ROWS, SUBLANES):
                kk = a8 + ph - first_tap
                if 0 <= kk < C_KSIZE:
                    acc = acc + shift_scr[ph, a8:a8 + rb, :] * cw_ref[kk:kk + 1, :]
        mu = jnp.mean(acc, axis=-1, keepdims=True)
        var = jnp.mean(jnp.square(acc - mu), axis=-1, keepdims=True)
        yn = (acc - mu) * lax.rsqrt(var + EPS) * lg_ref[...] + lb_ref[...]
        yc_ref[0, pl.ds(r, rb), :] = _silu(yn).astype(BF16)
        d0 = dwin[:, :LANES]
        d1 = dwin[:, LANES:]
        p = PAD_ROWS
        s2 = d0[p - 1:p - 1 + rb] + d0[p:p + rb]
        s4 = s2 + d0[p - 2:p - 2 + rb] + d0[p + 1:p + 1 + rb]
        q2 = d1[0:rb + 32] + d1[1:rb + 33]
        q4 = q2[0:rb + 24] + q2[2:rb + 26]
        q8 = q4[0:rb + 16] + q4[4:rb + 20]
        s8 = q8[p - 4:p - 4 + rb]
        s16 = q8[p - 8:p - 8 + rb] + q8[p:p + rb]
        lane1 = lane[:, :LANES]
        tot = jnp.concatenate([jnp.where(lane1 < POOL_GROUP, s2, s4), jnp.where(lane1 < POOL_GROUP, s8, s16)],
                              axis=1)
        t = r + lax.broadcasted_iota(jnp.int32, (rb, 1), 0)
        cnt = (jnp.minimum(t + half, n) - jnp.maximum(t - half, 0)).astype(F32)
        pooled = tot / cnt - dwin[p:p + rb]
        yd_ref[0, pl.ds(r, rb), :] = (_dot(pooled.astype(BF16), pw_ref[...]) * ps_ref[...]).astype(BF16)
        return carry

    lax.fori_loop(0, n // rb, block, 0)


def _convpool(uc, ud, cw, cb, lng, lnb, pw_bd, pscale):
    b, n, _ = ud.shape
    seq = lambda w: pl.BlockSpec((1, n, w), lambda bi: (bi, 0, 0))
    row = pl.BlockSpec((1, GROUP_W), lambda bi: (0, 0))
    return pl.pallas_call(
        _convpool_kernel,
        grid=(b,),
        in_specs=[seq(2 * GROUP_W), seq(GROUP_W),
                  pl.BlockSpec((C_KSIZE, GROUP_W), lambda bi: (0, 0)), row, row, row,
                  pl.BlockSpec((GROUP_W, GROUP_W), lambda bi: (0, 0)), row],
        out_specs=[seq(GROUP_W), seq(GROUP_W)],
        out_shape=[jax.ShapeDtypeStruct((b, n, GROUP_W), BF16)] * 2,
        scratch_shapes=[pltpu.VMEM((n + PAD_ROWS + TAIL_ROWS, GROUP_W), F32)] * 2
                       + [pltpu.VMEM((SUBLANES, CONV_ROW_BLOCK + 2 * PAD_ROWS - SUBLANES, GROUP_W), F32)],
        compiler_params=_cparams("arbitrary"),
        name="conv_pool",
    )(uc, ud, cw, cb, lng, lnb, pw_bd, pscale)


def _tail_kernel(per_seq, final, *refs):
    y_refs = refs[0:12]
    x_ref, xp_ref, xn_ref = refs[12:15]
    wo_ref, wu_ref, dw_ref, db_ref, wd_ref, mod_ref, g_ref, fg_ref, o_ref = refs[15:24]
    ybuf, xbuf, hbuf, ubuf_a, ubuf_b = refs[24:]
    acc = o_ref
    tm = x_ref.shape[0]
    hr = BF16_ROWS
    i = pl.program_id(0)
    first = (i % per_seq) == 0
    last = (i % per_seq) == per_seq - 1

    for gi in range(N_MIXERS):
        y_ref, yp_ref, yn_ref = y_refs[3 * gi:3 * gi + 3]
        ybuf[gi, 0:hr, :] = yp_ref[...]
        ybuf[gi, hr:hr + tm, :] = y_ref[...]
        ybuf[gi, hr + tm:, :] = yn_ref[...]
    xbuf[0:hr, :] = xp_ref[...]
    xbuf[hr:hr + tm, :] = x_ref[...]
    xbuf[hr + tm:, :] = xn_ref[...]
    y = jnp.zeros(xbuf.shape, F32)
    for gi in range(N_MIXERS):
        y = y + _dot(ybuf[gi], wo_ref[gi * GROUP_W:(gi + 1) * GROUP_W, :])
    xbuf[...] = xbuf[...] + mod_ref[0, 2:3, :] * y
    h = _norm_mod(xbuf[...], g_ref[...], mod_ref[0, 3:4, :], mod_ref[0, 4:5, :]).astype(BF16)
    row = lax.broadcasted_iota(jnp.int32, (tm + 2 * hr, 1), 0)
    outside = (first & (row < hr)) | (last & (row >= hr + tm))
    hbuf[...] = jnp.where(outside, jnp.zeros_like(h), h)

    acc[...] = jnp.zeros_like(acc)
    tf = FFN_CHUNK
    nj = D_FF // tf

    def cols(j, half):
        return pl.ds(pl.multiple_of(half * D_FF + j * tf, LANES), tf)

    def up(j, buf):
        buf[:, 0:tf] = _dot(hbuf[...], wu_ref[:, cols(j, 0)])
        buf[:, tf:] = _dot(hbuf[...], wu_ref[:, cols(j, 1)])

    def mix(j, buf):
        w = jnp.concatenate([dw_ref[:, cols(j, 0)], dw_ref[:, cols(j, 1)]], axis=1)
        u = jnp.concatenate([db_ref[:, cols(j, 0)], db_ref[:, cols(j, 1)]], axis=1)
        for kk in range(FFN_KSIZE):
            u = u + buf[pl.ds(hr - FFN_KSIZE // 2 + kk, tm), :] * w[kk:kk + 1, :]
        act = (u[:, :tf] * _silu(u[:, tf:])).astype(BF16)
        acc[...] = acc[...] + _dot(act, wd_ref[pl.ds(pl.multiple_of(j * tf, tf), tf), :])

    up(0, ubuf_a)

    def pair(p, carry):
        up(2 * p + 1, ubuf_b)
        mix(2 * p, ubuf_a)
        up(2 * p + 2, ubuf_a)
        mix(2 * p + 1, ubuf_b)
        return carry

    lax.fori_loop(0, (nj - 1) // 2, pair, 0)
    mix(nj - 1, ubuf_a)
    x = xbuf[hr:hr + tm, :] + mod_ref[0, 5:6, :] * acc[...]
    if final:
        ms = jnp.mean(x * x, axis=-1, keepdims=True)
        x = x * lax.rsqrt(ms + EPS) * fg_ref[...]
    o_ref[...] = x


def _tail(ys, w_out_bf, wu_r, dw_r, db_r, wd_r, x2, mod, g2, final_g, seq, final):
    t = x2.shape[0]
    tm = min(TOKEN_TILE, seq)
    per_seq = seq // tm
    nblk = tm // BF16_ROWS
    nj = D_FF // FFN_CHUNK
    assert nj % 2 == 1
    tok = lambda i: (i, 0)
    prev = lambda i: (jnp.maximum(i * nblk - 1, 0), 0)
    nxt = lambda i: (jnp.minimum((i + 1) * nblk, t // BF16_ROWS - 1), 0)
    const = lambda shape: pl.BlockSpec(shape, lambda i: (0,) * len(shape), pipeline_mode=pl.Buffered(1))
    trio = lambda w: [pl.BlockSpec((tm, w), tok), pl.BlockSpec((BF16_ROWS, w), prev), pl.BlockSpec((BF16_ROWS, w), nxt)]
    halo_rows = tm + 2 * BF16_ROWS
    args = []
    for y in ys:
        args += [y, y, y]
    return pl.pallas_call(
        functools.partial(_tail_kernel, per_seq, final),
        grid=(t // tm,),
        in_specs=trio(GROUP_W) * N_MIXERS + trio(D_MODEL)
                 + [const((D_MODEL, D_MODEL)),
                    const((D_MODEL, 2 * D_FF)), const((FFN_KSIZE, 2 * D_FF)),
                    const((1, 2 * D_FF)), const((D_FF, D_MODEL)),
                    pl.BlockSpec((1, N_MOD, D_MODEL), lambda i: (i // per_seq, 0, 0)),
                    const((1, D_MODEL)), const((1, D_MODEL))],
        out_specs=pl.BlockSpec((tm, D_MODEL), tok),
        out_shape=jax.ShapeDtypeStruct((t, D_MODEL), F32),
        scratch_shapes=[pltpu.VMEM((N_MIXERS, halo_rows, GROUP_W), BF16),
                        pltpu.VMEM((halo_rows, D_MODEL), F32),
                        pltpu.VMEM((halo_rows, D_MODEL), BF16),
                        pltpu.VMEM((halo_rows, 2 * FFN_CHUNK), F32),
                        pltpu.VMEM((halo_rows, 2 * FFN_CHUNK), F32)],
        compiler_params=_cparams("arbitrary"),
        name="out_ffn",
    )(*args, x2, x2, x2, w_out_bf, wu_r, dw_r, db_r, wd_r, mod, g2, final_g)


def kernel(x, c, ctx, c_ctx, w_mod, b_mod, norm1_g, norm2_g, w_in, w_out, diff_lambda, diff_subln_g, ret_decay,
           conv_dw_w, conv_dw_b, conv_ln_g, conv_ln_b, pool_w, pool_scale, ffn_w_up, ffn_dw_w, ffn_dw_b,
           ffn_w_down, final_g):
    b, n, d = x.shape
    nc = ctx.shape[1]
    depth = w_mod.shape[0]
    assert d == D_MODEL and b + 1 <= MOD_ROWS
    assert n % RET_CHUNK == 0 and nc % RET_CHUNK == 0 and n % min(TOKEN_TILE, n) == 0

    cond = jnp.zeros((MOD_ROWS, D_MODEL), F32).at[:b].set(c).at[b].set(c_ctx)
    mods = _modulation(cond, w_mod, b_mod)

    tab_x = _rope_tables(n, A_QKDIM // 4) + _rope_tables(n, B_DIM // 4)
    one, zero = jnp.ones((nc, LANES), F32), jnp.zeros((nc, LANES), F32)
    tab_c = (one, zero, one, zero)

    x2 = x.reshape(b * n, d)
    c2 = ctx.reshape(b * nc, d)
    row = lambda v: v.reshape(1, -1)
    g3 = lambda a, s: a.reshape(b, s, GROUP_W)
    zero_state = jnp.zeros((b, 2, GROUP_W, GROUP_W), F32)

    for l in range(depth):
        need_ctx = l < depth - 1
        lambda_init = 0.8 - 0.6 * math.exp(-0.3 * l)
        mod_x = mods[l, :b].reshape(b, N_MOD, D_MODEL)
        mod_c = jnp.broadcast_to(mods[l, b].reshape(1, N_MOD, D_MODEL), (b, N_MOD, D_MODEL))
        w_in_bf = w_in[l].astype(BF16)
        w_out_bf = w_out[l].astype(BF16)
        wu_r = ffn_w_up[l].astype(BF16)
        dw_r = ffn_dw_w[l]
        db_r = ffn_dw_b[l].reshape(1, -1)
        wd_r = ffn_w_down[l].astype(BF16)
        dec_lanes = jnp.repeat(ret_decay[l], B_DIM, axis=-1)
        pw_bd = jnp.zeros((GROUP_W, GROUP_W), F32)
        for gi in range(len(POOL_WINDOWS)):
            pw_bd = pw_bd.at[gi * POOL_GROUP:(gi + 1) * POOL_GROUP, gi * POOL_GROUP:(gi + 1) * POOL_GROUP].set(
                pool_w[l, gi])
        pw_bd = pw_bd.astype(BF16)
        mixer_params = (conv_dw_w[l], row(conv_dw_b[l]), row(conv_ln_g[l]), row(conv_ln_b[l]), pw_bd,
                        row(pool_scale[l]))

        pc = _in_proj(c2, mod_c, row(norm1_g[l]), w_in_bf, tab_c, nc, n + nc, n)
        aq_c, ak_part, av_part, bq_c, bk_c, bv_c, bg_c, uc_c, ud_c = pc
        px = _in_proj(x2, mod_x, row(norm1_g[l]), w_in_bf, tab_x, n, n + nc, 0, kv_bufs=(ak_part, av_part))
        aq_x, ak_all, av_all, bq_x, bk_x, bv_x, bg_x, uc_x, ud_x = px

        ya_x = _attention(g3(aq_x, n), ak_all, av_all, diff_lambda[l], diff_subln_g[l], lambda_init)
        yb_c, st_c = _retention(g3(bq_c, nc), g3(bk_c, nc), g3(bv_c, nc), g3(bg_c, nc), dec_lanes, zero_state)
        yb_x, _ = _retention(g3(bq_x, n), g3(bk_x, n), g3(bv_x, n), g3(bg_x, n), dec_lanes, st_c)
        yc_x, yd_x = _convpool(uc_x.reshape(b, n, 2 * GROUP_W), g3(ud_x, n), *mixer_params)
        flat = lambda a: a.reshape(-1, GROUP_W)
        x2 = _tail([flat(ya_x), flat(yb_x), flat(yc_x), flat(yd_x)], w_out_bf, wu_r, dw_r, db_r, wd_r, x2, mod_x,
                   row(norm2_g[l]), row(final_g), n, final=not need_ctx)

        if need_ctx:
            ya_c = _attention(g3(aq_c, nc), ak_all, av_all, diff_lambda[l], diff_subln_g[l], lambda_init,
                              key_rows=(n, nc))
            yc_c, yd_c = _convpool(uc_c.reshape(b, nc, 2 * GROUP_W), g3(ud_c, nc), *mixer_params)
            c2 = _tail([flat(ya_c), flat(yb_c), flat(yc_c), flat(yd_c)], w_out_bf, wu_r, dw_r, db_r, wd_r, c2,
                       mod_c, row(norm2_g[l]), row(final_g), nc, final=False)

    return x2.reshape(b, n, d)
```

```python
import functools
import math

import numpy as np
import jax
import jax.numpy as jnp
from jax import lax
from jax.experimental import pallas as pl
from jax.experimental.pallas import tpu as pltpu

F32 = jnp.float32
BF16 = jnp.bfloat16

D_MODEL = 1024
DEPTH = 4
GRID_W = 64
N_MIXERS = 4
GROUP_W = D_MODEL // N_MIXERS
A_HEADS = 4
A_VDIM = GROUP_W // A_HEADS
A_QKDIM = A_VDIM // 2
B_HEADS = 4
B_DIM = GROUP_W // B_HEADS
C_KSIZE = 31
POOL_WINDOWS = (2, 4, 8, 16)
POOL_GROUP = GROUP_W // len(POOL_WINDOWS)
D_FF = 2816
FFN_KSIZE = 3
RET_CHUNK = 256
ROPE_BASE = 10000.0
EPS = 1e-6
N_MOD = 6
OFF_AQ = 0
OFF_BQ = OFF_AQ + GROUP_W
OFF_BG = OFF_BQ + GROUP_W
OFF_C = OFF_BG + GROUP_W
OFF_D = OFF_C + 2 * GROUP_W
OFF_KV = OFF_D + GROUP_W
KV_AK = 0
KV_AV = GROUP_W
KV_BK = 2 * GROUP_W
KV_BV = 3 * GROUP_W
D_IN = OFF_KV + 4 * GROUP_W

LANES = 128
BF16_ROWS = 16
MOD_ROWS = 16
TOKEN_TILE = 1024
ATTN_Q_TILE = 256
ATTN_KEY_TILE = 256
ATTN_TILES_PER_ITER = 3
LOG2E = 1.4426950408889634
CONV_ROW_BLOCK = 256
SUBLANES = 8
PAD_ROWS = 16
TAIL_ROWS = 32
RET_BATCHES = 2
FFN_CHUNK = 256
VMEM_LIMIT = 58 * 1024 * 1024


def _cparams(*sem):
    return pltpu.CompilerParams(dimension_semantics=sem, vmem_limit_bytes=VMEM_LIMIT)


def _sigmoid(x):
    return 1.0 / (1.0 + jnp.exp(-x))


def _silu(x):
    return x * _sigmoid(x)


def _dot(a, b):
    return jnp.dot(a, b, preferred_element_type=F32)


def _dot_nt(a, b):
    return lax.dot_general(a, b, (((1,), (1,)), ((), ())), preferred_element_type=F32)


def _rope_tables(n, blk):
    lane = np.arange(LANES)
    freq = lane % blk
    is_a = (lane % (2 * blk)) < blk
    is_row = (lane % (4 * blk)) < 2 * blk
    inv = ROPE_BASE ** (-freq.astype(np.float64) / blk)
    t = np.arange(n)
    pos = np.where(is_row[None, :], (t // GRID_W)[:, None], (t % GRID_W)[:, None]).astype(np.float64)
    ang = pos * inv[None, :]
    cos = np.cos(ang)
    sin = np.sin(ang) * np.where(is_a, -1.0, 1.0)[None, :]
    return jnp.asarray(cos, F32), jnp.asarray(sin, F32)


def _rope(x, cos, sin, blk):
    lane = lax.broadcasted_iota(jnp.int32, (1, LANES), 1)
    is_a = (lane % (2 * blk)) < blk
    outs = []
    for j in range(GROUP_W // LANES):
        xh = x[:, j * LANES:(j + 1) * LANES]
        from_hi = pltpu.roll(xh, LANES - blk, axis=1)
        from_lo = pltpu.roll(xh, blk, axis=1)
        outs.append(xh * cos + jnp.where(is_a, from_hi, from_lo) * sin)
    return jnp.concatenate(outs, axis=1)


def _head_mean(a, width):
    r = lax.broadcasted_iota(jnp.int32, (GROUP_W, GROUP_W), 0) // width
    c = lax.broadcasted_iota(jnp.int32, (GROUP_W, GROUP_W), 1) // width
    seg = jnp.where(r == c, 1.0 / width, 0.0).astype(BF16)
    a1 = a.astype(BF16)
    r1 = a - a1.astype(F32)
    a2 = r1.astype(BF16)
    a3 = (r1 - a2.astype(F32)).astype(BF16)
    return _dot(a1, seg) + _dot(a2, seg) + _dot(a3, seg)


def _mod_kernel(s_ref, w_ref, b_ref, o_ref):
    s = _silu(s_ref[...])
    o_ref[0] = _dot(s.astype(BF16), w_ref[0].astype(BF16)) + b_ref[0]


def _modulation(cond, w_mod, b_mod):
    depth = w_mod.shape[0]
    return pl.pallas_call(
        _mod_kernel,
        grid=(depth, N_MOD),
        in_specs=[
            pl.BlockSpec((MOD_ROWS, D_MODEL), lambda l, j: (0, 0)),
            pl.BlockSpec((1, D_MODEL, D_MODEL), lambda l, j: (l, 0, j)),
            pl.BlockSpec((1, 1, D_MODEL), lambda l, j: (l, 0, j)),
        ],
        out_specs=pl.BlockSpec((1, MOD_ROWS, D_MODEL), lambda l, j: (l, 0, j)),
        out_shape=jax.ShapeDtypeStruct((depth, MOD_ROWS, N_MOD * D_MODEL), F32),
        compiler_params=_cparams("arbitrary", "arbitrary"),
        name="modulation",
    )(cond, w_mod, b_mod.reshape(depth, 1, N_MOD * D_MODEL))


def _norm_mod(x, g, shift, scale):
    ms = jnp.mean(x * x, axis=-1, keepdims=True)
    return (x * lax.rsqrt(ms + EPS) * g) * (1.0 + scale) + shift


def _in_kernel(kv_rows, x_ref, mod_ref, g_ref, w_ref, ca_ref, sa_ref, cb_ref, sb_ref, *refs):
    aq_ref, ak_ref, av_ref, bq_ref, bk_ref, bv_ref, bg_ref, uc_ref, ud_ref = refs[-9:]
    h = _norm_mod(x_ref[...], g_ref[...], mod_ref[0, 0:1, :], mod_ref[0, 1:2, :]).astype(BF16)

    def proj(off, width=GROUP_W):
        return _dot(h, w_ref[:, off:off + width])

    ca, sa, cb, sb = ca_ref[...], sa_ref[...], cb_ref[...], sb_ref[...]
    aq_ref[...] = (_rope(proj(OFF_AQ), ca, sa, A_QKDIM // 4) * (A_QKDIM ** -0.5 * LOG2E)).astype(BF16)
    ak = _rope(proj(OFF_KV + KV_AK), ca, sa, A_QKDIM // 4).astype(BF16)
    av = proj(OFF_KV + KV_AV).astype(BF16)
    if kv_rows is None:
        ak_ref[0] = ak
        av_ref[0] = av
    else:
        off, cnt = kv_rows
        for ref, val in ((ak_ref, ak), (av_ref, av)):
            ref[0] = jnp.zeros(ref.shape[1:], BF16)
            ref[0, off:off + cnt, :] = val
    bq_ref[...] = _rope(proj(OFF_BQ), cb, sb, B_DIM // 4)
    bk_ref[...] = _rope(proj(OFF_KV + KV_BK), cb, sb, B_DIM // 4) * (B_DIM ** -0.5)
    bv_ref[...] = proj(OFF_KV + KV_BV).astype(BF16)
    bg_ref[...] = proj(OFF_BG)
    uc_ref[...] = proj(OFF_C, 2 * GROUP_W)
    ud_ref[...] = proj(OFF_D)


def _in_proj(x2, mod, g, w_bf, tables, seq, kv_rows, kv_offset, kv_bufs=None):
    t = x2.shape[0]
    b = t // seq
    tm = min(TOKEN_TILE, seq)
    per_seq = seq // tm
    assert kv_offset % tm == 0
    tok = lambda i: (i, 0)
    tab = pl.BlockSpec((tm, LANES), lambda i: (i % per_seq, 0))
    gw = lambda dt: jax.ShapeDtypeStruct((t, GROUP_W), dt)
    kv = jax.ShapeDtypeStruct((b, kv_rows, GROUP_W), BF16)
    gspec = pl.BlockSpec((tm, GROUP_W), tok)
    if kv_bufs is None:
        assert per_seq == 1
        kv_rows_arg = (kv_offset, tm)
        kvspec = pl.BlockSpec((1, kv_rows, GROUP_W), lambda i: (i, 0, 0))
    else:
        kv_rows_arg = None
        kvspec = pl.BlockSpec((1, tm, GROUP_W), lambda i: (i // per_seq, i % per_seq + kv_offset // tm, 0))
    in_specs = [
        pl.BlockSpec((tm, D_MODEL), tok),
        pl.BlockSpec((1, N_MOD, D_MODEL), lambda i: (i // per_seq, 0, 0)),
        pl.BlockSpec((1, D_MODEL), lambda i: (0, 0)),
        pl.BlockSpec((D_MODEL, D_IN), lambda i: (0, 0)),
        tab, tab, tab, tab,
    ]
    args = [x2, mod, g, w_bf, *tables]
    aliases = {}
    if kv_bufs is not None:
        aliases = {len(args): 1, len(args) + 1: 2}
        in_specs += [pl.BlockSpec(memory_space=pl.ANY)] * 2
        args += list(kv_bufs)
    return pl.pallas_call(
        functools.partial(_in_kernel, kv_rows_arg),
        grid=(t // tm,),
        in_specs=in_specs,
        out_specs=[gspec, kvspec, kvspec] + [gspec] * 4 + [pl.BlockSpec((tm, 2 * GROUP_W), tok), gspec],
        out_shape=[gw(BF16), kv, kv, gw(F32), gw(F32), gw(BF16), gw(F32),
                   jax.ShapeDtypeStruct((t, 2 * GROUP_W), F32), gw(F32)],
        input_output_aliases=aliases,
        compiler_params=_cparams("arbitrary"),
        name="in_proj",
    )(*args)


def _attn_kernel(lambda_init, q_ref, k_ref, v_ref, lam_ref, g_ref, o_ref,
                 qs_scr, s_a, s_b, mx_a, mx_b, l_a, l_b, acc_a, acc_b):
    nq = q_ref.shape[1]
    tq = min(ATTN_Q_TILE, nq)
    n_qt = nq // tq
    n_maps = 2 * A_HEADS
    kt = ATTN_KEY_TILE
    n_tiles = k_ref.shape[1] // kt
    per_iter = ATTN_TILES_PER_ITER if n_tiles % ATTN_TILES_PER_ITER == 0 else 1
    n_iter = n_tiles // per_iter
    lp = lam_ref[...]
    lam = (jnp.exp(jnp.sum(lp[0:1] * lp[1:2], axis=-1, keepdims=True))
           - jnp.exp(jnp.sum(lp[2:3] * lp[3:4], axis=-1, keepdims=True)) + lambda_init)
    lane = lax.broadcasted_iota(jnp.int32, (1, GROUP_W), 1)

    def q_rows(t):
        return pl.ds(pl.multiple_of(t * tq, tq), tq)

    def stack_queries(t):
        q = q_ref[0, q_rows(t), :]
        for u in range(n_maps):
            keep = (lane >= u * A_QKDIM) & (lane < (u + 1) * A_QKDIM)
            qs_scr[u * tq:(u + 1) * tq, :] = jnp.where(keep, q, jnp.zeros_like(q))

    def scores(j, s_scr, mx_scr, first):
        for t in range(per_iter):
            tile = j * per_iter + t
            r = tile * kt if isinstance(tile, int) else pl.multiple_of(tile * kt, kt)
            s = _dot_nt(qs_scr[...], k_ref[0, pl.ds(r, kt), :])
            s_scr[tile] = s
            m = jnp.maximum(s[:, :LANES], s[:, LANES:])
            mx_scr[...] = m if first and t == 0 else jnp.maximum(mx_scr[...], m)

    def values(j, s_scr, mx_scr, l_scr, acc_scr, first):
        for t in range(per_iter):
            tile = j * per_iter + t
            r = tile * kt if isinstance(tile, int) else pl.multiple_of(tile * kt, kt)
            m = mx_scr[...]
            e = jnp.exp2(s_scr[tile] - jnp.concatenate([m, m], axis=1))
            pv = _dot(e.astype(BF16), v_ref[0, pl.ds(r, kt), :])
            if first and t == 0:
                l_scr[...] = e[:, :LANES] + e[:, LANES:]
                acc_scr[...] = pv
            else:
                l_scr[...] = l_scr[...] + (e[:, :LANES] + e[:, LANES:])
                acc_scr[...] = acc_scr[...] + pv

    def end_scores(mx_scr):
        mx_scr[...] = jnp.broadcast_to(jnp.max(mx_scr[...], axis=-1, keepdims=True), mx_scr.shape)

    def end_values(t, l_scr, acc_scr):
        o = jnp.zeros((tq, GROUP_W), F32)
        for hd in range(A_HEADS):
            parts = []
            for m in range(2):
                u = 2 * hd + m
                rows = slice(u * tq, (u + 1) * tq)
                parts.append(acc_scr[rows, :] * (1.0 / jnp.sum(l_scr[rows, :], axis=-1, keepdims=True)))
            o = jnp.where((lane >= hd * A_VDIM) & (lane < (hd + 1) * A_VDIM), parts[0] - lam * parts[1], o)
        y = o * lax.rsqrt(_head_mean(o * o, A_VDIM) + EPS) * g_ref[...] * (1.0 - lambda_init)
        o_ref[0, q_rows(t), :] = y.astype(BF16)

    bufs = ((s_a, mx_a, l_a, acc_a), (s_b, mx_b, l_b, acc_b))

    def phase(t, parity, has_prev, has_next):
        s_cur, mx_cur, l_cur, acc_cur = bufs[parity]
        s_nxt, mx_nxt, l_prev, acc_prev = bufs[1 - parity]
        end_scores(mx_cur)
        if has_next:
            stack_queries(t + 1)
            scores(0, s_nxt, mx_nxt, True)
        values(0, s_cur, mx_cur, l_cur, acc_cur, True)
        if has_prev:
            end_values(t - 1, l_prev, acc_prev)

        def body(j, carry):
            if has_next:
                scores(j, s_nxt, mx_nxt, False)
            values(j, s_cur, mx_cur, l_cur, acc_cur, False)
            return carry

        lax.fori_loop(1, n_iter, body, 0)

    assert n_qt == 1 or n_qt % 2 == 0
    stack_queries(0)
    scores(0, s_a, mx_a, True)
    lax.fori_loop(1, n_iter, lambda j, c: (scores(j, s_a, mx_a, False), c)[1], 0)
    if n_qt == 1:
        phase(0, 0, False, False)
        end_values(0, l_a, acc_a)
    else:
        phase(0, 0, False, True)

        def pair(p, carry):
            phase(2 * p + 1, 1, True, True)
            phase(2 * p + 2, 0, True, True)
            return carry

        lax.fori_loop(0, (n_qt - 2) // 2, pair, 0)
        phase(n_qt - 1, 1, True, False)
        end_values(n_qt - 1, l_b, acc_b)


def _attention(q, k, v, lam_p, subln_g, lambda_init, key_rows=None):
    b, nq, _ = q.shape
    k_off, nk = (0, k.shape[1]) if key_rows is None else key_rows
    assert k_off % nk == 0
    rows = 2 * A_HEADS * min(ATTN_Q_TILE, nq)
    seq = lambda n, blk=0, **kw: pl.BlockSpec((1, n, GROUP_W), lambda bi: (bi, blk, 0), **kw)
    once = dict(pipeline_mode=pl.Buffered(1))
    s_shape = pltpu.VMEM((nk // ATTN_KEY_TILE, rows, ATTN_KEY_TILE), F32)
    return pl.pallas_call(
        functools.partial(_attn_kernel, lambda_init),
        grid=(b,),
        in_specs=[seq(nq), seq(nk, k_off // nk, **once), seq(nk, k_off // nk, **once),
                  pl.BlockSpec((4, A_QKDIM), lambda bi: (0, 0)),
                  pl.BlockSpec((1, GROUP_W), lambda bi: (0, 0))],
        out_specs=seq(nq),
        out_shape=jax.ShapeDtypeStruct((b, nq, GROUP_W), BF16),
        scratch_shapes=[pltpu.VMEM((rows, GROUP_W), BF16), s_shape, s_shape,
                        pltpu.VMEM((rows, LANES), F32), pltpu.VMEM((rows, LANES), F32),
                        pltpu.VMEM((rows, LANES), F32), pltpu.VMEM((rows, LANES), F32),
                        pltpu.VMEM((rows, GROUP_W), F32), pltpu.VMEM((rows, GROUP_W), F32)],
        compiler_params=_cparams("arbitrary"),
        name="diff_attention",
    )(q, k, v, lam_p, jnp.tile(subln_g, A_HEADS).reshape(1, GROUP_W))


def _ret_kernel(q_ref, k_ref, v_ref, gate_ref, dec_ref, s0_ref, y_ref, st_ref, o_scr, s_scr, intra_scr, dec_scr):
    nb, n = q_ref.shape[0], q_ref.shape[1]
    c = RET_CHUNK
    nch = n // c
    lane = lax.broadcasted_iota(jnp.int32, (1, GROUP_W), 1)
    head_masks = [(lane >= h * B_DIM) & (lane < (h + 1) * B_DIM) for h in range(B_HEADS)]
    rr = lax.broadcasted_iota(jnp.int32, (GROUP_W, GROUP_W), 0) // B_DIM
    cc = lax.broadcasted_iota(jnp.int32, (GROUP_W, GROUP_W), 1) // B_DIM
    block_diag = rr == cc
    pos = lax.broadcasted_iota(jnp.int32, (c, 1), 0).astype(F32)
    ii = lax.broadcasted_iota(jnp.int32, (c, c), 0)
    jj = lax.broadcasted_iota(jnp.int32, (c, c), 1)

    c_decs = []
    for d in range(2):
        lg = -jnp.exp(dec_ref[d:d + 1, :])
        rel = (ii - jj) if d == 0 else (jj - ii)
        relf = jnp.maximum(rel, 0).astype(F32)
        for h in range(B_HEADS):
            intra_scr[d, h * c:(h + 1) * c, :] = jnp.where(
                rel >= 0, jnp.exp(relf * lg[:, h * B_DIM:h * B_DIM + 1]), 0.0)
        if d == 0:
            dec_scr[d, 0] = jnp.exp((pos + 1.0) * lg)
            dec_scr[d, 1] = jnp.exp((c - 1.0 - pos) * lg)
        else:
            dec_scr[d, 0] = jnp.exp((c - pos) * lg)
            dec_scr[d, 1] = jnp.exp(pos * lg)
        c_decs.append(jnp.exp(float(c) * lg))
    s_scr[...] = s0_ref[...]

    def chunk(i, carry):
        for bi in range(nb):
            for d in range(2):
                ci = i if d == 0 else nch - 1 - i
                r = pl.multiple_of(ci * c, c)
                qc = q_ref[bi, pl.ds(r, c), :]
                kc = k_ref[bi, pl.ds(r, c), :]
                vc = v_ref[bi, pl.ds(r, c), :]
                qb = qc.astype(BF16)
                q4 = jnp.concatenate([jnp.where(mk, qb, jnp.zeros_like(qb)) for mk in head_masks], axis=0)
                sc = (_dot_nt(q4, kc.astype(BF16)) * intra_scr[d]).astype(BF16)
                state = s_scr[bi, d]
                out = _dot((qc * dec_scr[d, 0]).astype(BF16), state.astype(BF16))
                for h in range(B_HEADS):
                    out = out + jnp.where(head_masks[h], _dot(sc[h * c:(h + 1) * c], vc), 0.0)
                o_scr[bi, d, pl.ds(r, c), :] = out
                kd_t = (kc * dec_scr[d, 1]).T.astype(BF16)
                s_scr[bi, d] = c_decs[d] * state + jnp.where(block_diag, _dot(kd_t, vc), 0.0)
        return carry

    lax.fori_loop(0, nch, chunk, 0)
    st_ref[...] = s_scr[...]

    def finish(i, carry):
        r = pl.multiple_of(i * c, c)
        for bi in range(nb):
            o = o_scr[bi, 0, pl.ds(r, c), :] + o_scr[bi, 1, pl.ds(r, c), :]
            y = _silu(gate_ref[bi, pl.ds(r, c), :]) * (o * lax.rsqrt(_head_mean(o * o, B_DIM) + EPS))
            y_ref[bi, pl.ds(r, c), :] = y.astype(BF16)
        return carry

    lax.fori_loop(0, nch, finish, 0)


def _retention(q, k, v, gate, dec_lanes, s0):
    b, n, _ = q.shape
    nb = RET_BATCHES if b % RET_BATCHES == 0 else 1
    seq = pl.BlockSpec((nb, n, GROUP_W), lambda bi: (bi, 0, 0))
    st = pl.BlockSpec((nb, 2, GROUP_W, GROUP_W), lambda bi: (bi, 0, 0, 0))
    return pl.pallas_call(
        _ret_kernel,
        grid=(b // nb,),
        in_specs=[seq, seq, seq, seq, pl.BlockSpec((2, GROUP_W), lambda bi: (0, 0)), st],
        out_specs=[seq, st],
        out_shape=[jax.ShapeDtypeStruct((b, n, GROUP_W), BF16),
                   jax.ShapeDtypeStruct((b, 2, GROUP_W, GROUP_W), F32)],
        scratch_shapes=[pltpu.VMEM((nb, 2, n, GROUP_W), F32), pltpu.VMEM((nb, 2, GROUP_W, GROUP_W), F32),
                        pltpu.VMEM((2, B_HEADS * RET_CHUNK, RET_CHUNK), F32),
                        pltpu.VMEM((2, 2, RET_CHUNK, GROUP_W), F32)],
        compiler_params=_cparams("arbitrary"),
        name="retention",
    )(q, k, v, gate, dec_lanes, s0)


def _convpool_kernel(uc_ref, ud_ref, cw_ref, cb_ref, lg_ref, lb_ref, pw_ref, ps_ref, yc_ref, yd_ref,
                     ubuf, dbuf, shift_scr):
    n = uc_ref.shape[1]
    rb = CONV_ROW_BLOCK
    for buf in (ubuf, dbuf):
        buf[0:PAD_ROWS, :] = jnp.zeros((PAD_ROWS, GROUP_W), F32)
        buf[PAD_ROWS + n:, :] = jnp.zeros((TAIL_ROWS, GROUP_W), F32)

    def fill(i, carry):
        r = pl.multiple_of(i * RET_CHUNK, RET_CHUNK)
        u2 = uc_ref[0, pl.ds(r, RET_CHUNK), :]
        ubuf[pl.ds(PAD_ROWS + r, RET_CHUNK), :] = u2[:, :GROUP_W] * _sigmoid(u2[:, GROUP_W:])
        dbuf[pl.ds(PAD_ROWS + r, RET_CHUNK), :] = ud_ref[0, pl.ds(r, RET_CHUNK), :]
        return carry

    lax.fori_loop(0, n // RET_CHUNK, fill, 0)

    lane = lax.broadcasted_iota(jnp.int32, (1, GROUP_W), 1)
    half = jnp.left_shift(1, lane // POOL_GROUP)
    first_tap = PAD_ROWS - C_KSIZE // 2

    def block(i, carry):
        r = pl.multiple_of(i * rb, rb)
        uwin = ubuf[pl.ds(r, rb + 2 * PAD_ROWS), :]
        dwin = dbuf[pl.ds(r, rb + PAD_ROWS + TAIL_ROWS), :]
        acc = jnp.zeros((rb, GROUP_W), F32) + cb_ref[...]
        for ph in range(SUBLANES):
            shift_scr[ph] = uwin[ph:ph + rb + 2 * PAD_ROWS - SUBLANES]
            for a8 in range(0, 2 * PAD_ROWS, SUBLANES):
                kk = a8 + ph - first_tap
                if 0 <= kk < C_KSIZE:
                    acc = acc + shift_scr[ph, a8:a8 + rb, :] * cw_ref[kk:kk + 1, :]
        mu = jnp.mean(acc, axis=-1, keepdims=True)
        var = jnp.mean(jnp.square(acc - mu), axis=-1, keepdims=True)
        yn = (acc - mu) * lax.rsqrt(var + EPS) * lg_ref[...] + lb_ref[...]
        yc_ref[0, pl.ds(r, rb), :] = _silu(yn).astype(BF16)
        d0 = dwin[:, :LANES]
        d1 = dwin[:, LANES:]
        p = PAD_ROWS
        s2 = d0[p - 1:p - 1 + rb] + d0[p:p + rb]
        s4 = s2 + d0[p - 2:p - 2 + rb] + d0[p + 1:p + 1 + rb]
        q2 = d1[0:rb + 32] + d1[1:rb + 33]
        q4 = q2[0:rb + 24] + q2[2:rb + 26]
        q8 = q4[0:rb + 16] + q4[4:rb + 20]
        s8 = q8[p - 4:p - 4 + rb]
        s16 = q8[p - 8:p - 8 + rb] + q8[p:p + rb]
        lane1 = lane[:, :LANES]
        tot = jnp.concatenate([jnp.where(lane1 < POOL_GROUP, s2, s4), jnp.where(lane1 < POOL_GROUP, s8, s16)],
                              axis=1)
        t = r + lax.broadcasted_iota(jnp.int32, (rb, 1), 0)
        cnt = (jnp.minimum(t + half, n) - jnp.maximum(t - half, 0)).astype(F32)
        pooled = tot / cnt - dwin[p:p + rb]
        yd_ref[0, pl.ds(r, rb), :] = (_dot(pooled.astype(BF16), pw_ref[...]) * ps_ref[...]).astype(BF16)
        return carry

    lax.fori_loop(0, n // rb, block, 0)


def _convpool(uc, ud, cw, cb, lng, lnb, pw_bd, pscale):
    b, n, _ = ud.shape
    seq = lambda w: pl.BlockSpec((1, n, w), lambda bi: (bi, 0, 0))
    row = pl.BlockSpec((1, GROUP_W), lambda bi: (0, 0))
    return pl.pallas_call(
        _convpool_kernel,
        grid=(b,),
        in_specs=[seq(2 * GROUP_W), seq(GROUP_W),
                  pl.BlockSpec((C_KSIZE, GROUP_W), lambda bi: (0, 0)), row, row, row,
                  pl.BlockSpec((GROUP_W, GROUP_W), lambda bi: (0, 0)), row],
        out_specs=[seq(GROUP_W), seq(GROUP_W)],
        out_shape=[jax.ShapeDtypeStruct((b, n, GROUP_W), BF16)] * 2,
        scratch_shapes=[pltpu.VMEM((n + PAD_ROWS + TAIL_ROWS, GROUP_W), F32)] * 2
                       + [pltpu.VMEM((SUBLANES, CONV_ROW_BLOCK + 2 * PAD_ROWS - SUBLANES, GROUP_W), F32)],
        compiler_params=_cparams("arbitrary"),
        name="conv_pool",
    )(uc, ud, cw, cb, lng, lnb, pw_bd, pscale)


def _tail_kernel(per_seq, final, *refs):
    y_refs = refs[0:12]
    x_ref, xp_ref, xn_ref = refs[12:15]
    wo_ref, wu_ref, dw_ref, db_ref, wd_ref, mod_ref, g_ref, fg_ref, o_ref = refs[15:24]
    ybuf, xbuf, hbuf, ubuf_a, ubuf_b = refs[24:]
    acc = o_ref
    tm = x_ref.shape[0]
    hr = BF16_ROWS
    i = pl.program_id(0)
    first = (i % per_seq) == 0
    last = (i % per_seq) == per_seq - 1

    for gi in range(N_MIXERS):
        y_ref, yp_ref, yn_ref = y_refs[3 * gi:3 * gi + 3]
        ybuf[gi, 0:hr, :] = yp_ref[...]
        ybuf[gi, hr:hr + tm, :] = y_ref[...]
        ybuf[gi, hr + tm:, :] = yn_ref[...]
    xbuf[0:hr, :] = xp_ref[...]
    xbuf[hr:hr + tm, :] = x_ref[...]
    xbuf[hr + tm:, :] = xn_ref[...]
    y = jnp.zeros(xbuf.shape, F32)
    for gi in range(N_MIXERS):
        y = y + _dot(ybuf[gi], wo_ref[gi * GROUP_W:(gi + 1) * GROUP_W, :])
    xbuf[...] = xbuf[...] + mod_ref[0, 2:3, :] * y
    h = _norm_mod(xbuf[...], g_ref[...], mod_ref[0, 3:4, :], mod_ref[0, 4:5, :]).astype(BF16)
    row = lax.broadcasted_iota(jnp.int32, (tm + 2 * hr, 1), 0)
    outside = (first & (row < hr)) | (last & (row >= hr + tm))
    hbuf[...] = jnp.where(outside, jnp.zeros_like(h), h)

    acc[...] = jnp.zeros_like(acc)
    tf = FFN_CHUNK
    nj = D_FF // tf

    def cols(j, half):
        return pl.ds(pl.multiple_of(half * D_FF + j * tf, LANES), tf)

    def up(j, buf):
        buf[:, 0:tf] = _dot(hbuf[...], wu_ref[:, cols(j, 0)])
        buf[:, tf:] = _dot(hbuf[...], wu_ref[:, cols(j, 1)])

    def mix(j, buf):
        w = jnp.concatenate([dw_ref[:, cols(j, 0)], dw_ref[:, cols(j, 1)]], axis=1)
        u = jnp.concatenate([db_ref[:, cols(j, 0)], db_ref[:, cols(j, 1)]], axis=1)
        for kk in range(FFN_KSIZE):
            u = u + buf[pl.ds(hr - FFN_KSIZE // 2 + kk, tm), :] * w[kk:kk + 1, :]
        act = (u[:, :tf] * _silu(u[:, tf:])).astype(BF16)
        acc[...] = acc[...] + _dot(act, wd_ref[pl.ds(pl.multiple_of(j * tf, tf), tf), :])

    up(0, ubuf_a)

    def pair(p, carry):
        up(2 * p + 1, ubuf_b)
        mix(2 * p, ubuf_a)
        up(2 * p + 2, ubuf_a)
        mix(2 * p + 1, ubuf_b)
        return carry

    lax.fori_loop(0, (nj - 1) // 2, pair, 0)
    mix(nj - 1, ubuf_a)
    x = xbuf[hr:hr + tm, :] + mod_ref[0, 5:6, :] * acc[...]
    if final:
        ms = jnp.mean(x * x, axis=-1, keepdims=True)
        x = x * lax.rsqrt(ms + EPS) * fg_ref[...]
    o_ref[...] = x


def _tail(ys, w_out_bf, wu_r, dw_r, db_r, wd_r, x2, mod, g2, final_g, seq, final):
    t = x2.shape[0]
    tm = min(TOKEN_TILE, seq)
    per_seq = seq // tm
    nblk = tm // BF16_ROWS
    nj = D_FF // FFN_CHUNK
    assert nj % 2 == 1
    tok = lambda i: (i, 0)
    prev = lambda i: (jnp.maximum(i * nblk - 1, 0), 0)
    nxt = lambda i: (jnp.minimum((i + 1) * nblk, t // BF16_ROWS - 1), 0)
    const = lambda shape: pl.BlockSpec(shape, lambda i: (0,) * len(shape), pipeline_mode=pl.Buffered(1))
    trio = lambda w: [pl.BlockSpec((tm, w), tok), pl.BlockSpec((BF16_ROWS, w), prev), pl.BlockSpec((BF16_ROWS, w), nxt)]
    halo_rows = tm + 2 * BF16_ROWS
    args = []
    for y in ys:
        args += [y, y, y]
    return pl.pallas_call(
        functools.partial(_tail_kernel, per_seq, final),
        grid=(t // tm,),
        in_specs=trio(GROUP_W) * N_MIXERS + trio(D_MODEL)
                 + [const((D_MODEL, D_MODEL)),
                    const((D_MODEL, 2 * D_FF)), const((FFN_KSIZE, 2 * D_FF)),
                    const((1, 2 * D_FF)), const((D_FF, D_MODEL)),
                    pl.BlockSpec((1, N_MOD, D_MODEL), lambda i: (i // per_seq, 0, 0)),
                    const((1, D_MODEL)), const((1, D_MODEL))],
        out_specs=pl.BlockSpec((tm, D_MODEL), tok),
        out_shape=jax.ShapeDtypeStruct((t, D_MODEL), F32),
        scratch_shapes=[pltpu.VMEM((N_MIXERS, halo_rows, GROUP_W), BF16),
                        pltpu.VMEM((halo_rows, D_MODEL), F32),
                        pltpu.VMEM((halo_rows, D_MODEL), BF16),
                        pltpu.VMEM((halo_rows, 2 * FFN_CHUNK), F32),
                        pltpu.VMEM((halo_rows, 2 * FFN_CHUNK), F32)],
        compiler_params=_cparams("arbitrary"),
        name="out_ffn",
    )(*args, x2, x2, x2, w_out_bf, wu_r, dw_r, db_r, wd_r, mod, g2, final_g)


def kernel(x, c, ctx, c_ctx, w_mod, b_mod, norm1_g, norm2_g, w_in, w_out, diff_lambda, diff_subln_g, ret_decay,
           conv_dw_w, conv_dw_b, conv_ln_g, conv_ln_b, pool_w, pool_scale, ffn_w_up, ffn_dw_w, ffn_dw_b,
           ffn_w_down, final_g):
    b, n, d = x.shape
    nc = ctx.shape[1]
    depth = w_mod.shape[0]
    assert d == D_MODEL and b + 1 <= MOD_ROWS
    assert n % RET_CHUNK == 0 and nc % RET_CHUNK == 0 and n % min(TOKEN_TILE, n) == 0

    cond = jnp.zeros((MOD_ROWS, D_MODEL), F32).at[:b].set(c).at[b].set(c_ctx)
    mods = _modulation(cond, w_mod, b_mod)

    tab_x = _rope_tables(n, A_QKDIM // 4) + _rope_tables(n, B_DIM // 4)
    one, zero = jnp.ones((nc, LANES), F32), jnp.zeros((nc, LANES), F32)
    tab_c = (one, zero, one, zero)

    x2 = x.reshape(b * n, d)
    c2 = ctx.reshape(b * nc, d)
    row = lambda v: v.reshape(1, -1)
    g3 = lambda a, s: a.reshape(b, s, GROUP_W)
    zero_state = jnp.zeros((b, 2, GROUP_W, GROUP_W), F32)

    for l in range(depth):
        need_ctx = l < depth - 1
        lambda_init = 0.8 - 0.6 * math.exp(-0.3 * l)
        mod_x = mods[l, :b].reshape(b, N_MOD, D_MODEL)
        mod_c = jnp.broadcast_to(mods[l, b].reshape(1, N_MOD, D_MODEL), (b, N_MOD, D_MODEL))
        w_in_bf = w_in[l].astype(BF16)
        w_out_bf = w_out[l].astype(BF16)
        wu_r = ffn_w_up[l].astype(BF16)
        dw_r = ffn_dw_w[l]
        db_r = ffn_dw_b[l].reshape(1, -1)
        wd_r = ffn_w_down[l].astype(BF16)
        dec_lanes = jnp.repeat(ret_decay[l], B_DIM, axis=-1)
        pw_bd = jnp.zeros((GROUP_W, GROUP_W), F32)
        for gi in range(len(POOL_WINDOWS)):
            pw_bd = pw_bd.at[gi * POOL_GROUP:(gi + 1) * POOL_GROUP, gi * POOL_GROUP:(gi + 1) * POOL_GROUP].set(
                pool_w[l, gi])
        pw_bd = pw_bd.astype(BF16)
        mixer_params = (conv_dw_w[l], row(conv_dw_b[l]), row(conv_ln_g[l]), row(conv_ln_b[l]), pw_bd,
                        row(pool_scale[l]))

        pc = _in_proj(c2, mod_c, row(norm1_g[l]), w_in_bf, tab_c, nc, n + nc, n)
        aq_c, ak_part, av_part, bq_c, bk_c, bv_c, bg_c, uc_c, ud_c = pc
        px = _in_proj(x2, mod_x, row(norm1_g[l]), w_in_bf, tab_x, n, n + nc, 0, kv_bufs=(ak_part, av_part))
        aq_x, ak_all, av_all, bq_x, bk_x, bv_x, bg_x, uc_x, ud_x = px

        ya_x = _attention(g3(aq_x, n), ak_all, av_all, diff_lambda[l], diff_subln_g[l], lambda_init)
        yb_c, st_c = _retention(g3(bq_c, nc), g3(bk_c, nc), g3(bv_c, nc), g3(bg_c, nc), dec_lanes, zero_state)
        yb_x, _ = _retention(g3(bq_x, n), g3(bk_x, n), g3(bv_x, n), g3(bg_x, n), dec_lanes, st_c)
        yc_x, yd_x = _convpool(uc_x.reshape(b, n, 2 * GROUP_W), g3(ud_x, n), *mixer_params)
        flat = lambda a: a.reshape(-1, GROUP_W)
        x2 = _tail([flat(ya_x), flat(yb_x), flat(yc_x), flat(yd_x)], w_out_bf, wu_r, dw_r, db_r, wd_r, x2, mod_x,
                   row(norm2_g[l]), row(final_g), n, final=not need_ctx)

        if need_ctx:
            ya_c = _attention(g3(aq_c, nc), ak_all, av_all, diff_lambda[l], diff_subln_g[l], lambda_init,
                              key_rows=(n, nc))
            yc_c, yd_c = _convpool(uc_c.reshape(b, nc, 2 * GROUP_W), g3(ud_c, nc), *mixer_params)
            c2 = _tail([flat(ya_c), flat(yb_c), flat(yc_c), flat(yd_c)], w_out_bf, wu_r, dw_r, db_r, wd_r, c2,
                       mod_c, row(norm2_g[l]), row(final_g), nc, final=False)

    return x2.reshape(b, n, d)
```

```python
import functools
import math

import numpy as np
import jax
import jax.numpy as jnp
from jax import lax
from jax.experimental import pallas as pl
from jax.experimental.pallas import tpu as pltpu

F32 = jnp.float32
BF16 = jnp.bfloat16

D_MODEL = 1024
DEPTH = 4
GRID_W = 64
N_MIXERS = 4
GROUP_W = D_MODEL // N_MIXERS
A_HEADS = 4
A_VDIM = GROUP_W // A_HEADS
A_QKDIM = A_VDIM // 2
B_HEADS = 4
B_DIM = GROUP_W // B_HEADS
C_KSIZE = 31
POOL_WINDOWS = (2, 4, 8, 16)
POOL_GROUP = GROUP_W // len(POOL_WINDOWS)
D_FF = 2816
FFN_KSIZE = 3
RET_CHUNK = 256
ROPE_BASE = 10000.0
EPS = 1e-6
N_MOD = 6
OFF_AQ = 0
OFF_BQ = OFF_AQ + GROUP_W
OFF_BG = OFF_BQ + GROUP_W
OFF_C = OFF_BG + GROUP_W
OFF_D = OFF_C + 2 * GROUP_W
OFF_KV = OFF_D + GROUP_W
KV_AK = 0
KV_AV = GROUP_W
KV_BK = 2 * GROUP_W
KV_BV = 3 * GROUP_W
D_IN = OFF_KV + 4 * GROUP_W

LANES = 128
BF16_ROWS = 16
MOD_ROWS = 16
TOKEN_TILE = 1024
ATTN_Q_TILE = 256
ATTN_KEY_TILE = 256
ATTN_TILES_PER_ITER = 3
LOG2E = 1.4426950408889634
CONV_ROW_BLOCK = 256
SUBLANES = 8
PAD_ROWS = 16
TAIL_ROWS = 32
RET_BATCHES = 2
FFN_CHUNK = 256
VMEM_LIMIT = 58 * 1024 * 1024


def _cparams(*sem):
    return pltpu.CompilerParams(dimension_semantics=sem, vmem_limit_bytes=VMEM_LIMIT)


def _sigmoid(x):
    return 1.0 / (1.0 + jnp.exp(-x))


def _silu(x):
    return x * _sigmoid(x)


def _dot(a, b):
    return jnp.dot(a, b, preferred_element_type=F32)


def _dot_nt(a, b):
    return lax.dot_general(a, b, (((1,), (1,)), ((), ())), preferred_element_type=F32)


def _rope_tables(n, blk):
    lane = np.arange(LANES)
    freq = lane % blk
    is_a = (lane % (2 * blk)) < blk
    is_row = (lane % (4 * blk)) < 2 * blk
    inv = ROPE_BASE ** (-freq.astype(np.float64) / blk)
    t = np.arange(n)
    pos = np.where(is_row[None, :], (t // GRID_W)[:, None], (t % GRID_W)[:, None]).astype(np.float64)
    ang = pos * inv[None, :]
    cos = np.cos(ang)
    sin = np.sin(ang) * np.where(is_a, -1.0, 1.0)[None, :]
    return jnp.asarray(cos, F32), jnp.asarray(sin, F32)


def _rope(x, cos, sin, blk):
    lane = lax.broadcasted_iota(jnp.int32, (1, LANES), 1)
    is_a = (lane % (2 * blk)) < blk
    outs = []
    for j in range(GROUP_W // LANES):
        xh = x[:, j * LANES:(j + 1) * LANES]
        from_hi = pltpu.roll(xh, LANES - blk, axis=1)
        from_lo = pltpu.roll(xh, blk, axis=1)
        outs.append(xh * cos + jnp.where(is_a, from_hi, from_lo) * sin)
    return jnp.concatenate(outs, axis=1)


def _head_mean(a, width):
    r = lax.broadcasted_iota(jnp.int32, (GROUP_W, GROUP_W), 0) // width
    c = lax.broadcasted_iota(jnp.int32, (GROUP_W, GROUP_W), 1) // width
    seg = jnp.where(r == c, 1.0 / width, 0.0).astype(BF16)
    a1 = a.astype(BF16)
    r1 = a - a1.astype(F32)
    a2 = r1.astype(BF16)
    a3 = (r1 - a2.astype(F32)).astype(BF16)
    return _dot(a1, seg) + _dot(a2, seg) + _dot(a3, seg)


def _mod_kernel(s_ref, w_ref, b_ref, o_ref):
    s = _silu(s_ref[...])
    o_ref[0] = _dot(s.astype(BF16), w_ref[0].astype(BF16)) + b_ref[0]


def _modulation(cond, w_mod, b_mod):
    depth = w_mod.shape[0]
    return pl.pallas_call(
        _mod_kernel,
        grid=(depth, N_MOD),
        in_specs=[
            pl.BlockSpec((MOD_ROWS, D_MODEL), lambda l, j: (0, 0)),
            pl.BlockSpec((1, D_MODEL, D_MODEL), lambda l, j: (l, 0, j)),
            pl.BlockSpec((1, 1, D_MODEL), lambda l, j: (l, 0, j)),
        ],
        out_specs=pl.BlockSpec((1, MOD_ROWS, D_MODEL), lambda l, j: (l, 0, j)),
        out_shape=jax.ShapeDtypeStruct((depth, MOD_ROWS, N_MOD * D_MODEL), F32),
        compiler_params=_cparams("arbitrary", "arbitrary"),
        name="modulation",
    )(cond, w_mod, b_mod.reshape(depth, 1, N_MOD * D_MODEL))


def _norm_mod(x, g, shift, scale):
    ms = jnp.mean(x * x, axis=-1, keepdims=True)
    return (x * lax.rsqrt(ms + EPS) * g) * (1.0 + scale) + shift


def _in_kernel(kv_rows, x_ref, mod_ref, g_ref, w_ref, ca_ref, sa_ref, cb_ref, sb_ref, *refs):
    aq_ref, ak_ref, av_ref, bq_ref, bk_ref, bv_ref, bg_ref, uc_ref, ud_ref = refs[-9:]
    h = _norm_mod(x_ref[...], g_ref[...], mod_ref[0, 0:1, :], mod_ref[0, 1:2, :]).astype(BF16)

    def proj(off, width=GROUP_W):
        return _dot(h, w_ref[0, :, off:off + width])

    ca, sa, cb, sb = ca_ref[...], sa_ref[...], cb_ref[...], sb_ref[...]
    aq_ref[...] = (_rope(proj(OFF_AQ), ca, sa, A_QKDIM // 4) * (A_QKDIM ** -0.5 * LOG2E)).astype(BF16)
    ak = _rope(proj(OFF_KV + KV_AK), ca, sa, A_QKDIM // 4).astype(BF16)
    av = proj(OFF_KV + KV_AV).astype(BF16)
    if kv_rows is None:
        ak_ref[0] = ak
        av_ref[0] = av
    else:
        off, cnt = kv_rows
        for ref, val in ((ak_ref, ak), (av_ref, av)):
            ref[0] = jnp.zeros(ref.shape[1:], BF16)
            ref[0, off:off + cnt, :] = val
    bq_ref[...] = _rope(proj(OFF_BQ), cb, sb, B_DIM // 4)
    bk_ref[...] = _rope(proj(OFF_KV + KV_BK), cb, sb, B_DIM // 4) * (B_DIM ** -0.5)
    bv_ref[...] = proj(OFF_KV + KV_BV).astype(BF16)
    bg_ref[...] = proj(OFF_BG)
    uc_ref[...] = proj(OFF_C, 2 * GROUP_W)
    ud_ref[...] = proj(OFF_D)


def _in_proj(x2, mod, g, w_bf, layer, tables, seq, kv_rows, kv_offset, kv_bufs=None):
    t = x2.shape[0]
    b = t // seq
    tm = min(TOKEN_TILE, seq)
    per_seq = seq // tm
    assert kv_offset % tm == 0
    tok = lambda i: (i, 0)
    tab = pl.BlockSpec((tm, LANES), lambda i: (i % per_seq, 0))
    gw = lambda dt: jax.ShapeDtypeStruct((t, GROUP_W), dt)
    kv = jax.ShapeDtypeStruct((b, kv_rows, GROUP_W), BF16)
    gspec = pl.BlockSpec((tm, GROUP_W), tok)
    if kv_bufs is None:
        assert per_seq == 1
        kv_rows_arg = (kv_offset, tm)
        kvspec = pl.BlockSpec((1, kv_rows, GROUP_W), lambda i: (i, 0, 0))
    else:
        kv_rows_arg = None
        kvspec = pl.BlockSpec((1, tm, GROUP_W), lambda i: (i // per_seq, i % per_seq + kv_offset // tm, 0))
    in_specs = [
        pl.BlockSpec((tm, D_MODEL), tok),
        pl.BlockSpec((1, N_MOD, D_MODEL), lambda i: (i // per_seq, 0, 0)),
        pl.BlockSpec((1, D_MODEL), lambda i: (0, 0)),
        pl.BlockSpec((1, D_MODEL, D_IN), lambda i: (layer, 0, 0)),
        tab, tab, tab, tab,
    ]
    args = [x2, mod, g, w_bf, *tables]
    aliases = {}
    if kv_bufs is not None:
        aliases = {len(args): 1, len(args) + 1: 2}
        in_specs += [pl.BlockSpec(memory_space=pl.ANY)] * 2
        args += list(kv_bufs)
    return pl.pallas_call(
        functools.partial(_in_kernel, kv_rows_arg),
        grid=(t // tm,),
        in_specs=in_specs,
        out_specs=[gspec, kvspec, kvspec] + [gspec] * 4 + [pl.BlockSpec((tm, 2 * GROUP_W), tok), gspec],
        out_shape=[gw(BF16), kv, kv, gw(F32), gw(F32), gw(BF16), gw(F32),
                   jax.ShapeDtypeStruct((t, 2 * GROUP_W), F32), gw(F32)],
        input_output_aliases=aliases,
        compiler_params=_cparams("arbitrary"),
        name="in_proj",
    )(*args)


def _attn_kernel(lambda_init, q_ref, k_ref, v_ref, lam_ref, g_ref, o_ref,
                 qs_scr, s_a, s_b, mx_a, mx_b, l_a, l_b, acc_a, acc_b):
    nq = q_ref.shape[1]
    tq = min(ATTN_Q_TILE, nq)
    n_qt = nq // tq
    n_maps = 2 * A_HEADS
    kt = ATTN_KEY_TILE
    n_tiles = k_ref.shape[1] // kt
    per_iter = ATTN_TILES_PER_ITER if n_tiles % ATTN_TILES_PER_ITER == 0 else 1
    n_iter = n_tiles // per_iter
    lp = lam_ref[...]
    lam = (jnp.exp(jnp.sum(lp[0:1] * lp[1:2], axis=-1, keepdims=True))
           - jnp.exp(jnp.sum(lp[2:3] * lp[3:4], axis=-1, keepdims=True)) + lambda_init)
    lane = lax.broadcasted_iota(jnp.int32, (1, GROUP_W), 1)

    def q_rows(t):
        return pl.ds(pl.multiple_of(t * tq, tq), tq)

    def stack_queries(t):
        q = q_ref[0, q_rows(t), :]
        for u in range(n_maps):
            keep = (lane >= u * A_QKDIM) & (lane < (u + 1) * A_QKDIM)
            qs_scr[u * tq:(u + 1) * tq, :] = jnp.where(keep, q, jnp.zeros_like(q))

    def scores(j, s_scr, mx_scr, first):
        for t in range(per_iter):
            tile = j * per_iter + t
            r = tile * kt if isinstance(tile, int) else pl.multiple_of(tile * kt, kt)
            s = _dot_nt(qs_scr[...], k_ref[0, pl.ds(r, kt), :])
            s_scr[tile] = s
            m = jnp.maximum(s[:, :LANES], s[:, LANES:])
            mx_scr[...] = m if first and t == 0 else jnp.maximum(mx_scr[...], m)

    def values(j, s_scr, mx_scr, l_scr, acc_scr, first):
        for t in range(per_iter):
            tile = j * per_iter + t
            r = tile * kt if isinstance(tile, int) else pl.multiple_of(tile * kt, kt)
            m = mx_scr[...]
            e = jnp.exp2(s_scr[tile] - jnp.concatenate([m, m], axis=1))
            pv = _dot(e.astype(BF16), v_ref[0, pl.ds(r, kt), :])
            if first and t == 0:
                l_scr[...] = e[:, :LANES] + e[:, LANES:]
                acc_scr[...] = pv
            else:
                l_scr[...] = l_scr[...] + (e[:, :LANES] + e[:, LANES:])
                acc_scr[...] = acc_scr[...] + pv

    def end_scores(mx_scr):
        mx_scr[...] = jnp.broadcast_to(jnp.max(mx_scr[...], axis=-1, keepdims=True), mx_scr.shape)

    def end_values(t, l_scr, acc_scr):
        o = jnp.zeros((tq, GROUP_W), F32)
        for hd in range(A_HEADS):
            parts = []
            for m in range(2):
                u = 2 * hd + m
                rows = slice(u * tq, (u + 1) * tq)
                parts.append(acc_scr[rows, :] * (1.0 / jnp.sum(l_scr[rows, :], axis=-1, keepdims=True)))
            o = jnp.where((lane >= hd * A_VDIM) & (lane < (hd + 1) * A_VDIM), parts[0] - lam * parts[1], o)
        y = o * lax.rsqrt(_head_mean(o * o, A_VDIM) + EPS) * g_ref[...] * (1.0 - lambda_init)
        o_ref[0, q_rows(t), :] = y.astype(BF16)

    bufs = ((s_a, mx_a, l_a, acc_a), (s_b, mx_b, l_b, acc_b))

    def phase(t, parity, has_prev, has_next):
        s_cur, mx_cur, l_cur, acc_cur = bufs[parity]
        s_nxt, mx_nxt, l_prev, acc_prev = bufs[1 - parity]
        end_scores(mx_cur)
        if has_next:
            stack_queries(t + 1)
            scores(0, s_nxt, mx_nxt, True)
        values(0, s_cur, mx_cur, l_cur, acc_cur, True)
        if has_prev:
            end_values(t - 1, l_prev, acc_prev)

        def body(j, carry):
            if has_next:
                scores(j, s_nxt, mx_nxt, False)
            values(j, s_cur, mx_cur, l_cur, acc_cur, False)
            return carry

        lax.fori_loop(1, n_iter, body, 0)

    assert n_qt == 1 or n_qt % 2 == 0
    stack_queries(0)
    scores(0, s_a, mx_a, True)
    lax.fori_loop(1, n_iter, lambda j, c: (scores(j, s_a, mx_a, False), c)[1], 0)
    if n_qt == 1:
        phase(0, 0, False, False)
        end_values(0, l_a, acc_a)
    else:
        phase(0, 0, False, True)

        def pair(p, carry):
            phase(2 * p + 1, 1, True, True)
            phase(2 * p + 2, 0, True, True)
            return carry

        lax.fori_loop(0, (n_qt - 2) // 2, pair, 0)
        phase(n_qt - 1, 1, True, False)
        end_values(n_qt - 1, l_b, acc_b)


def _attention(q, k, v, lam_p, subln_g, lambda_init, key_rows=None):
    b, nq, _ = q.shape
    k_off, nk = (0, k.shape[1]) if key_rows is None else key_rows
    assert k_off % nk == 0
    rows = 2 * A_HEADS * min(ATTN_Q_TILE, nq)
    seq = lambda n, blk=0, **kw: pl.BlockSpec((1, n, GROUP_W), lambda bi: (bi, blk, 0), **kw)
    once = dict(pipeline_mode=pl.Buffered(1))
    s_shape = pltpu.VMEM((nk // ATTN_KEY_TILE, rows, ATTN_KEY_TILE), F32)
    return pl.pallas_call(
        functools.partial(_attn_kernel, lambda_init),
        grid=(b,),
        in_specs=[seq(nq), seq(nk, k_off // nk, **once), seq(nk, k_off // nk, **once),
                  pl.BlockSpec((4, A_QKDIM), lambda bi: (0, 0)),
                  pl.BlockSpec((1, GROUP_W), lambda bi: (0, 0))],
        out_specs=seq(nq),
        out_shape=jax.ShapeDtypeStruct((b, nq, GROUP_W), BF16),
        scratch_shapes=[pltpu.VMEM((rows, GROUP_W), BF16), s_shape, s_shape,
                        pltpu.VMEM((rows, LANES), F32), pltpu.VMEM((rows, LANES), F32),
                        pltpu.VMEM((rows, LANES), F32), pltpu.VMEM((rows, LANES), F32),
                        pltpu.VMEM((rows, GROUP_W), F32), pltpu.VMEM((rows, GROUP_W), F32)],
        compiler_params=_cparams("arbitrary"),
        name="diff_attention",
    )(q, k, v, lam_p, jnp.tile(subln_g, A_HEADS).reshape(1, GROUP_W))


def _ret_kernel(q_ref, k_ref, v_ref, gate_ref, dec_ref, s0_ref, y_ref, st_ref, o_scr, s_scr, intra_scr, dec_scr):
    nb, n = q_ref.shape[0], q_ref.shape[1]
    c = RET_CHUNK
    nch = n // c
    lane = lax.broadcasted_iota(jnp.int32, (1, GROUP_W), 1)
    head_masks = [(lane >= h * B_DIM) & (lane < (h + 1) * B_DIM) for h in range(B_HEADS)]
    rr = lax.broadcasted_iota(jnp.int32, (GROUP_W, GROUP_W), 0) // B_DIM
    cc = lax.broadcasted_iota(jnp.int32, (GROUP_W, GROUP_W), 1) // B_DIM
    block_diag = rr == cc
    pos = lax.broadcasted_iota(jnp.int32, (c, 1), 0).astype(F32)
    ii = lax.broadcasted_iota(jnp.int32, (c, c), 0)
    jj = lax.broadcasted_iota(jnp.int32, (c, c), 1)

    c_decs = []
    for d in range(2):
        lg = -jnp.exp(dec_ref[d:d + 1, :])
        rel = (ii - jj) if d == 0 else (jj - ii)
        relf = jnp.maximum(rel, 0).astype(F32)
        for h in range(B_HEADS):
            intra_scr[d, h * c:(h + 1) * c, :] = jnp.where(
                rel >= 0, jnp.exp(relf * lg[:, h * B_DIM:h * B_DIM + 1]), 0.0)
        if d == 0:
            dec_scr[d, 0] = jnp.exp((pos + 1.0) * lg)
            dec_scr[d, 1] = jnp.exp((c - 1.0 - pos) * lg)
        else:
            dec_scr[d, 0] = jnp.exp((c - pos) * lg)
            dec_scr[d, 1] = jnp.exp(pos * lg)
        c_decs.append(jnp.exp(float(c) * lg))
    s_scr[...] = s0_ref[...]

    def chunk(i, carry):
        for bi in range(nb):
            for d in range(2):
                ci = i if d == 0 else nch - 1 - i
                r = pl.multiple_of(ci * c, c)
                qc = q_ref[bi, pl.ds(r, c), :]
                kc = k_ref[bi, pl.ds(r, c), :]
                vc = v_ref[bi, pl.ds(r, c), :]
                qb = qc.astype(BF16)
                q4 = jnp.concatenate([jnp.where(mk, qb, jnp.zeros_like(qb)) for mk in head_masks], axis=0)
                sc = (_dot_nt(q4, kc.astype(BF16)) * intra_scr[d]).astype(BF16)
                state = s_scr[bi, d]
                out = _dot((qc * dec_scr[d, 0]).astype(BF16), state.astype(BF16))
                for h in range(B_HEADS):
                    out = out + jnp.where(head_masks[h], _dot(sc[h * c:(h + 1) * c], vc), 0.0)
                o_scr[bi, d, pl.ds(r, c), :] = out
                kd_t = (kc * dec_scr[d, 1]).T.astype(BF16)
                s_scr[bi, d] = c_decs[d] * state + jnp.where(block_diag, _dot(kd_t, vc), 0.0)
        return carry

    lax.fori_loop(0, nch, chunk, 0)
    st_ref[...] = s_scr[...]

    def finish(i, carry):
        r = pl.multiple_of(i * c, c)
        for bi in range(nb):
            o = o_scr[bi, 0, pl.ds(r, c), :] + o_scr[bi, 1, pl.ds(r, c), :]
            y = _silu(gate_ref[bi, pl.ds(r, c), :]) * (o * lax.rsqrt(_head_mean(o * o, B_DIM) + EPS))
            y_ref[bi, pl.ds(r, c), :] = y.astype(BF16)
        return carry

    lax.fori_loop(0, nch, finish, 0)


def _retention(q, k, v, gate, dec_lanes, s0):
    b, n, _ = q.shape
    nb = RET_BATCHES if b % RET_BATCHES == 0 else 1
    seq = pl.BlockSpec((nb, n, GROUP_W), lambda bi: (bi, 0, 0))
    st = pl.BlockSpec((nb, 2, GROUP_W, GROUP_W), lambda bi: (bi, 0, 0, 0))
    return pl.pallas_call(
        _ret_kernel,
        grid=(b // nb,),
        in_specs=[seq, seq, seq, seq, pl.BlockSpec((2, GROUP_W), lambda bi: (0, 0)), st],
        out_specs=[seq, st],
        out_shape=[jax.ShapeDtypeStruct((b, n, GROUP_W), BF16),
                   jax.ShapeDtypeStruct((b, 2, GROUP_W, GROUP_W), F32)],
        scratch_shapes=[pltpu.VMEM((nb, 2, n, GROUP_W), F32), pltpu.VMEM((nb, 2, GROUP_W, GROUP_W), F32),
                        pltpu.VMEM((2, B_HEADS * RET_CHUNK, RET_CHUNK), F32),
                        pltpu.VMEM((2, 2, RET_CHUNK, GROUP_W), F32)],
        compiler_params=_cparams("arbitrary"),
        name="retention",
    )(q, k, v, gate, dec_lanes, s0)


def _convpool_kernel(uc_ref, ud_ref, cw_ref, cb_ref, lg_ref, lb_ref, pw_ref, ps_ref, yc_ref, yd_ref,
                     ubuf, dbuf, shift_scr):
    n = uc_ref.shape[1]
    rb = CONV_ROW_BLOCK
    for buf in (ubuf, dbuf):
        buf[0:PAD_ROWS, :] = jnp.zeros((PAD_ROWS, GROUP_W), F32)
        buf[PAD_ROWS + n:, :] = jnp.zeros((TAIL_ROWS, GROUP_W), F32)

    def fill(i, carry):
        r = pl.multiple_of(i * RET_CHUNK, RET_CHUNK)
        u2 = uc_ref[0, pl.ds(r, RET_CHUNK), :]
        ubuf[pl.ds(PAD_ROWS + r, RET_CHUNK), :] = u2[:, :GROUP_W] * _sigmoid(u2[:, GROUP_W:])
        dbuf[pl.ds(PAD_ROWS + r, RET_CHUNK), :] = ud_ref[0, pl.ds(r, RET_CHUNK), :]
        return carry

    lax.fori_loop(0, n // RET_CHUNK, fill, 0)

    lane = lax.broadcasted_iota(jnp.int32, (1, GROUP_W), 1)
    half = jnp.left_shift(1, lane // POOL_GROUP)
    first_tap = PAD_ROWS - C_KSIZE // 2

    def block(i, carry):
        r = pl.multiple_of(i * rb, rb)
        uwin = ubuf[pl.ds(r, rb + 2 * PAD_ROWS), :]
        dwin = dbuf[pl.ds(r, rb + PAD_ROWS + TAIL_ROWS), :]
        acc = jnp.zeros((rb, GROUP_W), F32) + cb_ref[...]
        for ph in range(SUBLANES):
            shift_scr[ph] = uwin[ph:ph + rb + 2 * PAD_ROWS - SUBLANES]
            for a8 in range(0, 2 * PAD_ROWS, SUBLANES):
                kk = a8 + ph - first_tap
                if 0 <= kk < C_KSIZE:
                    acc = acc + shift_scr[ph, a8:a8 + rb, :] * cw_ref[kk:kk + 1, :]
        mu = jnp.mean(acc, axis=-1, keepdims=True)
        var = jnp.mean(jnp.square(acc - mu), axis=-1, keepdims=True)
        yn = (acc - mu) * lax.rsqrt(var + EPS) * lg_ref[...] + lb_ref[...]
        yc_ref[0, pl.ds(r, rb), :] = _silu(yn).astype(BF16)
        d0 = dwin[:, :LANES]
        d1 = dwin[:, LANES:]
        p = PAD_ROWS
        s2 = d0[p - 1:p - 1 + rb] + d0[p:p + rb]
        s4 = s2 + d0[p - 2:p - 2 + rb] + d0[p + 1:p + 1 + rb]
        q2 = d1[0:rb + 32] + d1[1:rb + 33]
        q4 = q2[0:rb + 24] + q2[2:rb + 26]
        q8 = q4[0:rb + 16] + q4[4:rb + 20]
        s8 = q8[p - 4:p - 4 + rb]
        s16 = q8[p - 8:p - 8 + rb] + q8[p:p + rb]
        lane1 = lane[:, :LANES]
        tot = jnp.concatenate([jnp.where(lane1 < POOL_GROUP, s2, s4), jnp.where(lane1 < POOL_GROUP, s8, s16)],
                              axis=1)
        t = r + lax.broadcasted_iota(jnp.int32, (rb, 1), 0)
        cnt = (jnp.minimum(t + half, n) - jnp.maximum(t - half, 0)).astype(F32)
        pooled = tot / cnt - dwin[p:p + rb]
        yd_ref[0, pl.ds(r, rb), :] = (_dot(pooled.astype(BF16), pw_ref[...]) * ps_ref[...]).astype(BF16)
        return carry

    lax.fori_loop(0, n // rb, block, 0)


def _convpool(uc, ud, cw, cb, lng, lnb, pw_bd, pscale):
    b, n, _ = ud.shape
    seq = lambda w: pl.BlockSpec((1, n, w), lambda bi: (bi, 0, 0))
    row = pl.BlockSpec((1, GROUP_W), lambda bi: (0, 0))
    return pl.pallas_call(
        _convpool_kernel,
        grid=(b,),
        in_specs=[seq(2 * GROUP_W), seq(GROUP_W),
                  pl.BlockSpec((C_KSIZE, GROUP_W), lambda bi: (0, 0)), row, row, row,
                  pl.BlockSpec((GROUP_W, GROUP_W), lambda bi: (0, 0)), row],
        out_specs=[seq(GROUP_W), seq(GROUP_W)],
        out_shape=[jax.ShapeDtypeStruct((b, n, GROUP_W), BF16)] * 2,
        scratch_shapes=[pltpu.VMEM((n + PAD_ROWS + TAIL_ROWS, GROUP_W), F32)] * 2
                       + [pltpu.VMEM((SUBLANES, CONV_ROW_BLOCK + 2 * PAD_ROWS - SUBLANES, GROUP_W), F32)],
        compiler_params=_cparams("arbitrary"),
        name="conv_pool",
    )(uc, ud, cw, cb, lng, lnb, pw_bd, pscale)


def _tail_kernel(per_seq, final, *refs):
    y_refs = refs[0:12]
    x_ref, xp_ref, xn_ref = refs[12:15]
    wo_ref, wu_ref, dw_ref, db_ref, wd_ref, mod_ref, g_ref, fg_ref, o_ref = refs[15:24]
    ybuf, xbuf, hbuf, ubuf_a, ubuf_b = refs[24:]
    acc = o_ref
    tm = x_ref.shape[0]
    hr = BF16_ROWS
    i = pl.program_id(0)
    first = (i % per_seq) == 0
    last = (i % per_seq) == per_seq - 1

    for gi in range(N_MIXERS):
        y_ref, yp_ref, yn_ref = y_refs[3 * gi:3 * gi + 3]
        ybuf[gi, 0:hr, :] = yp_ref[...]
        ybuf[gi, hr:hr + tm, :] = y_ref[...]
        ybuf[gi, hr + tm:, :] = yn_ref[...]
    xbuf[0:hr, :] = xp_ref[...]
    xbuf[hr:hr + tm, :] = x_ref[...]
    xbuf[hr + tm:, :] = xn_ref[...]
    y = jnp.zeros(xbuf.shape, F32)
    for gi in range(N_MIXERS):
        y = y + _dot(ybuf[gi], wo_ref[0, gi * GROUP_W:(gi + 1) * GROUP_W, :])
    xbuf[...] = xbuf[...] + mod_ref[0, 2:3, :] * y
    h = _norm_mod(xbuf[...], g_ref[...], mod_ref[0, 3:4, :], mod_ref[0, 4:5, :]).astype(BF16)
    row = lax.broadcasted_iota(jnp.int32, (tm + 2 * hr, 1), 0)
    outside = (first & (row < hr)) | (last & (row >= hr + tm))
    hbuf[...] = jnp.where(outside, jnp.zeros_like(h), h)

    acc[...] = jnp.zeros_like(acc)
    tf = FFN_CHUNK
    nj = D_FF // tf

    def cols(j, half):
        return pl.ds(pl.multiple_of(half * D_FF + j * tf, LANES), tf)

    def up(j, buf):
        buf[:, 0:tf] = _dot(hbuf[...], wu_ref[0, :, cols(j, 0)])
        buf[:, tf:] = _dot(hbuf[...], wu_ref[0, :, cols(j, 1)])

    def mix(j, buf):
        w = jnp.concatenate([dw_ref[:, cols(j, 0)], dw_ref[:, cols(j, 1)]], axis=1)
        u = jnp.concatenate([db_ref[:, cols(j, 0)], db_ref[:, cols(j, 1)]], axis=1)
        for kk in range(FFN_KSIZE):
            u = u + buf[pl.ds(hr - FFN_KSIZE // 2 + kk, tm), :] * w[kk:kk + 1, :]
        act = (u[:, :tf] * _silu(u[:, tf:])).astype(BF16)
        acc[...] = acc[...] + _dot(act, wd_ref[0, pl.ds(pl.multiple_of(j * tf, tf), tf), :])

    up(0, ubuf_a)

    def pair(p, carry):
        up(2 * p + 1, ubuf_b)
        mix(2 * p, ubuf_a)
        up(2 * p + 2, ubuf_a)
        mix(2 * p + 1, ubuf_b)
        return carry

    lax.fori_loop(0, (nj - 1) // 2, pair, 0)
    mix(nj - 1, ubuf_a)
    x = xbuf[hr:hr + tm, :] + mod_ref[0, 5:6, :] * acc[...]
    if final:
        ms = jnp.mean(x * x, axis=-1, keepdims=True)
        x = x * lax.rsqrt(ms + EPS) * fg_ref[...]
    o_ref[...] = x


def _tail(ys, w_out_bf, wu_r, dw_r, db_r, wd_r, layer, x2, mod, g2, final_g, seq, final):
    t = x2.shape[0]
    tm = min(TOKEN_TILE, seq)
    per_seq = seq // tm
    nblk = tm // BF16_ROWS
    nj = D_FF // FFN_CHUNK
    assert nj % 2 == 1
    tok = lambda i: (i, 0)
    prev = lambda i: (jnp.maximum(i * nblk - 1, 0), 0)
    nxt = lambda i: (jnp.minimum((i + 1) * nblk, t // BF16_ROWS - 1), 0)
    const = lambda shape: pl.BlockSpec(shape, lambda i: (0,) * len(shape), pipeline_mode=pl.Buffered(1))
    stacked = lambda shape: pl.BlockSpec((1,) + shape, lambda i: (layer,) + (0,) * len(shape),
                                         pipeline_mode=pl.Buffered(1))
    trio = lambda w: [pl.BlockSpec((tm, w), tok), pl.BlockSpec((BF16_ROWS, w), prev), pl.BlockSpec((BF16_ROWS, w), nxt)]
    halo_rows = tm + 2 * BF16_ROWS
    args = []
    for y in ys:
        args += [y, y, y]
    return pl.pallas_call(
        functools.partial(_tail_kernel, per_seq, final),
        grid=(t // tm,),
        in_specs=trio(GROUP_W) * N_MIXERS + trio(D_MODEL)
                 + [stacked((D_MODEL, D_MODEL)),
                    stacked((D_MODEL, 2 * D_FF)), const((FFN_KSIZE, 2 * D_FF)),
                    const((1, 2 * D_FF)), stacked((D_FF, D_MODEL)),
                    pl.BlockSpec((1, N_MOD, D_MODEL), lambda i: (i // per_seq, 0, 0)),
                    const((1, D_MODEL)), const((1, D_MODEL))],
        out_specs=pl.BlockSpec((tm, D_MODEL), tok),
        out_shape=jax.ShapeDtypeStruct((t, D_MODEL), F32),
        scratch_shapes=[pltpu.VMEM((N_MIXERS, halo_rows, GROUP_W), BF16),
                        pltpu.VMEM((halo_rows, D_MODEL), F32),
                        pltpu.VMEM((halo_rows, D_MODEL), BF16),
                        pltpu.VMEM((halo_rows, 2 * FFN_CHUNK), F32),
                        pltpu.VMEM((halo_rows, 2 * FFN_CHUNK), F32)],
        compiler_params=_cparams("arbitrary"),
        name="out_ffn",
    )(*args, x2, x2, x2, w_out_bf, wu_r, dw_r, db_r, wd_r, mod, g2, final_g)


def kernel(x, c, ctx, c_ctx, w_mod, b_mod, norm1_g, norm2_g, w_in, w_out, diff_lambda, diff_subln_g, ret_decay,
           conv_dw_w, conv_dw_b, conv_ln_g, conv_ln_b, pool_w, pool_scale, ffn_w_up, ffn_dw_w, ffn_dw_b,
           ffn_w_down, final_g):
    b, n, d = x.shape
    nc = ctx.shape[1]
    depth = w_mod.shape[0]
    assert d == D_MODEL and b + 1 <= MOD_ROWS
    assert n % RET_CHUNK == 0 and nc % RET_CHUNK == 0 and n % min(TOKEN_TILE, n) == 0

    cond = jnp.zeros((MOD_ROWS, D_MODEL), F32).at[:b].set(c).at[b].set(c_ctx)
    mods = _modulation(cond, w_mod, b_mod)

    tab_x = _rope_tables(n, A_QKDIM // 4) + _rope_tables(n, B_DIM // 4)
    one, zero = jnp.ones((nc, LANES), F32), jnp.zeros((nc, LANES), F32)
    tab_c = (one, zero, one, zero)

    x2 = x.reshape(b * n, d)
    c2 = ctx.reshape(b * nc, d)
    row = lambda v: v.reshape(1, -1)
    g3 = lambda a, s: a.reshape(b, s, GROUP_W)
    zero_state = jnp.zeros((b, 2, GROUP_W, GROUP_W), F32)

    w_in_bf, w_out_bf = w_in.astype(BF16), w_out.astype(BF16)
    wu_bf, wd_bf = ffn_w_up.astype(BF16), ffn_w_down.astype(BF16)

    for l in range(depth):
        need_ctx = l < depth - 1
        lambda_init = 0.8 - 0.6 * math.exp(-0.3 * l)
        mod_x = mods[l, :b].reshape(b, N_MOD, D_MODEL)
        mod_c = jnp.broadcast_to(mods[l, b].reshape(1, N_MOD, D_MODEL), (b, N_MOD, D_MODEL))
        dw_r = ffn_dw_w[l]
        db_r = ffn_dw_b[l].reshape(1, -1)
        dec_lanes = jnp.repeat(ret_decay[l], B_DIM, axis=-1)
        pw_bd = jnp.zeros((GROUP_W, GROUP_W), F32)
        for gi in range(len(POOL_WINDOWS)):
            pw_bd = pw_bd.at[gi * POOL_GROUP:(gi + 1) * POOL_GROUP, gi * POOL_GROUP:(gi + 1) * POOL_GROUP].set(
                pool_w[l, gi])
        pw_bd = pw_bd.astype(BF16)
        mixer_params = (conv_dw_w[l], row(conv_dw_b[l]), row(conv_ln_g[l]), row(conv_ln_b[l]), pw_bd,
                        row(pool_scale[l]))

        pc = _in_proj(c2, mod_c, row(norm1_g[l]), w_in_bf, l, tab_c, nc, n + nc, n)
        aq_c, ak_part, av_part, bq_c, bk_c, bv_c, bg_c, uc_c, ud_c = pc
        px = _in_proj(x2, mod_x, row(norm1_g[l]), w_in_bf, l, tab_x, n, n + nc, 0, kv_bufs=(ak_part, av_part))
        aq_x, ak_all, av_all, bq_x, bk_x, bv_x, bg_x, uc_x, ud_x = px

        ya_x = _attention(g3(aq_x, n), ak_all, av_all, diff_lambda[l], diff_subln_g[l], lambda_init)
        yb_c, st_c = _retention(g3(bq_c, nc), g3(bk_c, nc), g3(bv_c, nc), g3(bg_c, nc), dec_lanes, zero_state)
        yb_x, _ = _retention(g3(bq_x, n), g3(bk_x, n), g3(bv_x, n), g3(bg_x, n), dec_lanes, st_c)
        yc_x, yd_x = _convpool(uc_x.reshape(b, n, 2 * GROUP_W), g3(ud_x, n), *mixer_params)
        flat = lambda a: a.reshape(-1, GROUP_W)
        x2 = _tail([flat(ya_x), flat(yb_x), flat(yc_x), flat(yd_x)], w_out_bf, wu_bf, dw_r, db_r, wd_bf, l, x2,
                   mod_x, row(norm2_g[l]), row(final_g), n, final=not need_ctx)

        if need_ctx:
            ya_c = _attention(g3(aq_c, nc), ak_all, av_all, diff_lambda[l], diff_subln_g[l], lambda_init,
                              key_rows=(n, nc))
            yc_c, yd_c = _convpool(uc_c.reshape(b, nc, 2 * GROUP_W), g3(ud_c, nc), *mixer_params)
            c2 = _tail([flat(ya_c), flat(yb_c), flat(yc_c), flat(yd_c)], w_out_bf, wu_bf, dw_r, db_r, wd_bf, l, c2,
                       mod_c, row(norm2_g[l]), row(final_g), nc, final=False)

    return x2.reshape(b, n, d)
```

```python
import functools
import math

import numpy as np
import jax
import jax.numpy as jnp
from jax import lax
from jax.experimental import pallas as pl
from jax.experimental.pallas import tpu as pltpu

F32 = jnp.float32
BF16 = jnp.bfloat16

D_MODEL = 1024
DEPTH = 4
GRID_W = 64
N_MIXERS = 4
GROUP_W = D_MODEL // N_MIXERS
A_HEADS = 4
A_VDIM = GROUP_W // A_HEADS
A_QKDIM = A_VDIM // 2
B_HEADS = 4
B_DIM = GROUP_W // B_HEADS
C_KSIZE = 31
POOL_WINDOWS = (2, 4, 8, 16)
POOL_GROUP = GROUP_W // len(POOL_WINDOWS)
D_FF = 2816
FFN_KSIZE = 3
RET_CHUNK = 256
ROPE_BASE = 10000.0
EPS = 1e-6
N_MOD = 6
OFF_AQ = 0
OFF_BQ = OFF_AQ + GROUP_W
OFF_BG = OFF_BQ + GROUP_W
OFF_C = OFF_BG + GROUP_W
OFF_D = OFF_C + 2 * GROUP_W
OFF_KV = OFF_D + GROUP_W
KV_AK = 0
KV_AV = GROUP_W
KV_BK = 2 * GROUP_W
KV_BV = 3 * GROUP_W
D_IN = OFF_KV + 4 * GROUP_W

LANES = 128
BF16_ROWS = 16
MOD_ROWS = 16
TOKEN_TILE = 1024
ATTN_Q_TILE = 256
ATTN_KEY_TILE = 256
ATTN_TILES_PER_ITER = 3
LOG2E = 1.4426950408889634
CONV_ROW_BLOCK = 256
SUBLANES = 8
PAD_ROWS = 16
TAIL_ROWS = 32
RET_BATCHES = 2
FFN_CHUNK = 256
VMEM_LIMIT = 58 * 1024 * 1024


def _cparams(*sem):
    return pltpu.CompilerParams(dimension_semantics=sem, vmem_limit_bytes=VMEM_LIMIT)


def _sigmoid(x):
    return 1.0 / (1.0 + jnp.exp(-x))


def _silu(x):
    return x * _sigmoid(x)


def _dot(a, b):
    return jnp.dot(a, b, preferred_element_type=F32)


def _dot_nt(a, b):
    return lax.dot_general(a, b, (((1,), (1,)), ((), ())), preferred_element_type=F32)


def _rope_tables(n, blk):
    lane = np.arange(LANES)
    freq = lane % blk
    is_a = (lane % (2 * blk)) < blk
    is_row = (lane % (4 * blk)) < 2 * blk
    inv = ROPE_BASE ** (-freq.astype(np.float64) / blk)
    t = np.arange(n)
    pos = np.where(is_row[None, :], (t // GRID_W)[:, None], (t % GRID_W)[:, None]).astype(np.float64)
    ang = pos * inv[None, :]
    cos = np.cos(ang)
    sin = np.sin(ang) * np.where(is_a, -1.0, 1.0)[None, :]
    return jnp.asarray(cos, F32), jnp.asarray(sin, F32)


def _rope(x, cos, sin, blk):
    lane = lax.broadcasted_iota(jnp.int32, (1, LANES), 1)
    is_a = (lane % (2 * blk)) < blk
    outs = []
    for j in range(GROUP_W // LANES):
        xh = x[:, j * LANES:(j + 1) * LANES]
        from_hi = pltpu.roll(xh, LANES - blk, axis=1)
        from_lo = pltpu.roll(xh, blk, axis=1)
        outs.append(xh * cos + jnp.where(is_a, from_hi, from_lo) * sin)
    return jnp.concatenate(outs, axis=1)


def _head_mean(a, width):
    r = lax.broadcasted_iota(jnp.int32, (GROUP_W, GROUP_W), 0) // width
    c = lax.broadcasted_iota(jnp.int32, (GROUP_W, GROUP_W), 1) // width
    seg = jnp.where(r == c, 1.0 / width, 0.0).astype(BF16)
    a1 = a.astype(BF16)
    r1 = a - a1.astype(F32)
    a2 = r1.astype(BF16)
    a3 = (r1 - a2.astype(F32)).astype(BF16)
    return _dot(a1, seg) + _dot(a2, seg) + _dot(a3, seg)


def _mod_kernel(s_ref, w_ref, b_ref, o_ref):
    s = _silu(s_ref[...])
    o_ref[0] = _dot(s.astype(BF16), w_ref[0].astype(BF16)) + b_ref[0]


def _modulation(cond, w_mod, b_mod):
    depth = w_mod.shape[0]
    return pl.pallas_call(
        _mod_kernel,
        grid=(depth, N_MOD),
        in_specs=[
            pl.BlockSpec((MOD_ROWS, D_MODEL), lambda l, j: (0, 0)),
            pl.BlockSpec((1, D_MODEL, D_MODEL), lambda l, j: (l, 0, j)),
            pl.BlockSpec((1, 1, D_MODEL), lambda l, j: (l, 0, j)),
        ],
        out_specs=pl.BlockSpec((1, MOD_ROWS, D_MODEL), lambda l, j: (l, 0, j)),
        out_shape=jax.ShapeDtypeStruct((depth, MOD_ROWS, N_MOD * D_MODEL), F32),
        compiler_params=_cparams("arbitrary", "arbitrary"),
        name="modulation",
    )(cond, w_mod, b_mod.reshape(depth, 1, N_MOD * D_MODEL))


def _norm_mod(x, g, shift, scale):
    ms = jnp.mean(x * x, axis=-1, keepdims=True)
    return (x * lax.rsqrt(ms + EPS) * g) * (1.0 + scale) + shift


def _in_kernel(kv_rows, x_ref, mod_ref, g_ref, w_ref, ca_ref, sa_ref, cb_ref, sb_ref, *refs):
    aq_ref, ak_ref, av_ref, bq_ref, bk_ref, bv_ref, bg_ref, uc_ref, ud_ref = refs[-9:]
    h = _norm_mod(x_ref[...], g_ref[...], mod_ref[0, 0:1, :], mod_ref[0, 1:2, :]).astype(BF16)

    def proj(off, width=GROUP_W):
        return _dot(h, w_ref[0, :, off:off + width])

    ca, sa, cb, sb = ca_ref[...], sa_ref[...], cb_ref[...], sb_ref[...]
    aq_ref[...] = (_rope(proj(OFF_AQ), ca, sa, A_QKDIM // 4) * (A_QKDIM ** -0.5 * LOG2E)).astype(BF16)
    ak = _rope(proj(OFF_KV + KV_AK), ca, sa, A_QKDIM // 4).astype(BF16)
    av = proj(OFF_KV + KV_AV).astype(BF16)
    if kv_rows is None:
        ak_ref[0] = ak
        av_ref[0] = av
    else:
        off, cnt = kv_rows
        for ref, val in ((ak_ref, ak), (av_ref, av)):
            ref[0] = jnp.zeros(ref.shape[1:], BF16)
            ref[0, off:off + cnt, :] = val
    bq_ref[...] = _rope(proj(OFF_BQ), cb, sb, B_DIM // 4)
    bk_ref[...] = _rope(proj(OFF_KV + KV_BK), cb, sb, B_DIM // 4) * (B_DIM ** -0.5)
    bv_ref[...] = proj(OFF_KV + KV_BV).astype(BF16)
    bg_ref[...] = proj(OFF_BG)
    uc_ref[...] = proj(OFF_C, 2 * GROUP_W)
    ud_ref[...] = proj(OFF_D)


def _in_proj(x2, mod, g, w_bf, layer, tables, seq, kv_rows, kv_offset, kv_bufs=None):
    t = x2.shape[0]
    b = t // seq
    tm = min(TOKEN_TILE, seq)
    per_seq = seq // tm
    assert kv_offset % tm == 0
    tok = lambda i: (i, 0)
    tab = pl.BlockSpec((tm, LANES), lambda i: (i % per_seq, 0))
    gw = lambda dt: jax.ShapeDtypeStruct((t, GROUP_W), dt)
    kv = jax.ShapeDtypeStruct((b, kv_rows, GROUP_W), BF16)
    gspec = pl.BlockSpec((tm, GROUP_W), tok)
    if kv_bufs is None:
        assert per_seq == 1
        kv_rows_arg = (kv_offset, tm)
        kvspec = pl.BlockSpec((1, kv_rows, GROUP_W), lambda i: (i, 0, 0))
    else:
        kv_rows_arg = None
        kvspec = pl.BlockSpec((1, tm, GROUP_W), lambda i: (i // per_seq, i % per_seq + kv_offset // tm, 0))
    in_specs = [
        pl.BlockSpec((tm, D_MODEL), tok),
        pl.BlockSpec((1, N_MOD, D_MODEL), lambda i: (i // per_seq, 0, 0)),
        pl.BlockSpec((1, D_MODEL), lambda i: (0, 0)),
        pl.BlockSpec((1, D_MODEL, D_IN), lambda i: (layer, 0, 0)),
        tab, tab, tab, tab,
    ]
    args = [x2, mod, g, w_bf, *tables]
    aliases = {}
    if kv_bufs is not None:
        aliases = {len(args): 1, len(args) + 1: 2}
        in_specs += [pl.BlockSpec(memory_space=pl.ANY)] * 2
        args += list(kv_bufs)
    return pl.pallas_call(
        functools.partial(_in_kernel, kv_rows_arg),
        grid=(t // tm,),
        in_specs=in_specs,
        out_specs=[gspec, kvspec, kvspec] + [gspec] * 4 + [pl.BlockSpec((tm, 2 * GROUP_W), tok), gspec],
        out_shape=[gw(BF16), kv, kv, gw(F32), gw(F32), gw(BF16), gw(F32),
                   jax.ShapeDtypeStruct((t, 2 * GROUP_W), F32), gw(F32)],
        input_output_aliases=aliases,
        compiler_params=_cparams("arbitrary"),
        name="in_proj",
    )(*args)


def _attn_kernel(lambda_init, q_ref, k_ref, v_ref, lam_ref, g_ref, o_ref,
                 qs_scr, s_a, s_b, mx_a, mx_b, l_a, l_b, acc_a, acc_b):
    nq = q_ref.shape[1]
    tq = min(ATTN_Q_TILE, nq)
    n_qt = nq // tq
    n_maps = 2 * A_HEADS
    kt = ATTN_KEY_TILE
    n_tiles = k_ref.shape[1] // kt
    per_iter = ATTN_TILES_PER_ITER if n_tiles % ATTN_TILES_PER_ITER == 0 else 1
    n_iter = n_tiles // per_iter
    lp = lam_ref[...]
    lam = (jnp.exp(jnp.sum(lp[0:1] * lp[1:2], axis=-1, keepdims=True))
           - jnp.exp(jnp.sum(lp[2:3] * lp[3:4], axis=-1, keepdims=True)) + lambda_init)
    lane = lax.broadcasted_iota(jnp.int32, (1, GROUP_W), 1)

    def q_rows(t):
        return pl.ds(pl.multiple_of(t * tq, tq), tq)

    def stack_queries(t):
        q = q_ref[0, q_rows(t), :]
        for u in range(n_maps):
            keep = (lane >= u * A_QKDIM) & (lane < (u + 1) * A_QKDIM)
            qs_scr[u * tq:(u + 1) * tq, :] = jnp.where(keep, q, jnp.zeros_like(q))

    def scores(j, s_scr, mx_scr, first):
        for t in range(per_iter):
            tile = j * per_iter + t
            r = tile * kt if isinstance(tile, int) else pl.multiple_of(tile * kt, kt)
            s = _dot_nt(qs_scr[...], k_ref[0, pl.ds(r, kt), :])
            s_scr[tile] = s
            m = jnp.maximum(s[:, :LANES], s[:, LANES:])
            mx_scr[...] = m if first and t == 0 else jnp.maximum(mx_scr[...], m)

    def values(j, s_scr, mx_scr, l_scr, acc_scr, first):
        for t in range(per_iter):
            tile = j * per_iter + t
            r = tile * kt if isinstance(tile, int) else pl.multiple_of(tile * kt, kt)
            m = mx_scr[...]
            e = jnp.exp2(s_scr[tile] - jnp.concatenate([m, m], axis=1))
            pv = _dot(e.astype(BF16), v_ref[0, pl.ds(r, kt), :])
            if first and t == 0:
                l_scr[...] = e[:, :LANES] + e[:, LANES:]
                acc_scr[...] = pv
            else:
                l_scr[...] = l_scr[...] + (e[:, :LANES] + e[:, LANES:])
                acc_scr[...] = acc_scr[...] + pv

    def end_scores(mx_scr):
        mx_scr[...] = jnp.broadcast_to(jnp.max(mx_scr[...], axis=-1, keepdims=True), mx_scr.shape)

    def end_values(t, l_scr, acc_scr):
        o = jnp.zeros((tq, GROUP_W), F32)
        for hd in range(A_HEADS):
            parts = []
            for m in range(2):
                u = 2 * hd + m
                rows = slice(u * tq, (u + 1) * tq)
                parts.append(acc_scr[rows, :] * (1.0 / jnp.sum(l_scr[rows, :], axis=-1, keepdims=True)))
            o = jnp.where((lane >= hd * A_VDIM) & (lane < (hd + 1) * A_VDIM), parts[0] - lam * parts[1], o)
        y = o * lax.rsqrt(_head_mean(o * o, A_VDIM) + EPS) * g_ref[...] * (1.0 - lambda_init)
        o_ref[0, q_rows(t), :] = y.astype(BF16)

    bufs = ((s_a, mx_a, l_a, acc_a), (s_b, mx_b, l_b, acc_b))

    def phase(t, parity, has_prev, has_next):
        s_cur, mx_cur, l_cur, acc_cur = bufs[parity]
        s_nxt, mx_nxt, l_prev, acc_prev = bufs[1 - parity]
        end_scores(mx_cur)
        if has_next:
            stack_queries(t + 1)
            scores(0, s_nxt, mx_nxt, True)
        values(0, s_cur, mx_cur, l_cur, acc_cur, True)
        if has_prev:
            end_values(t - 1, l_prev, acc_prev)

        def body(j, carry):
            if has_next:
                scores(j, s_nxt, mx_nxt, False)
            values(j, s_cur, mx_cur, l_cur, acc_cur, False)
            return carry

        lax.fori_loop(1, n_iter, body, 0)

    assert n_qt == 1 or n_qt % 2 == 0
    stack_queries(0)
    scores(0, s_a, mx_a, True)
    lax.fori_loop(1, n_iter, lambda j, c: (scores(j, s_a, mx_a, False), c)[1], 0)
    if n_qt == 1:
        phase(0, 0, False, False)
        end_values(0, l_a, acc_a)
    else:
        phase(0, 0, False, True)

        def pair(p, carry):
            phase(2 * p + 1, 1, True, True)
            phase(2 * p + 2, 0, True, True)
            return carry

        lax.fori_loop(0, (n_qt - 2) // 2, pair, 0)
        phase(n_qt - 1, 1, True, False)
        end_values(n_qt - 1, l_b, acc_b)


def _attention(q, k, v, lam_p, subln_g, lambda_init, key_rows=None):
    b, nq, _ = q.shape
    k_off, nk = (0, k.shape[1]) if key_rows is None else key_rows
    assert k_off % nk == 0
    rows = 2 * A_HEADS * min(ATTN_Q_TILE, nq)
    seq = lambda n, blk=0, **kw: pl.BlockSpec((1, n, GROUP_W), lambda bi: (bi, blk, 0), **kw)
    once = dict(pipeline_mode=pl.Buffered(1))
    s_shape = pltpu.VMEM((nk // ATTN_KEY_TILE, rows, ATTN_KEY_TILE), F32)
    return pl.pallas_call(
        functools.partial(_attn_kernel, lambda_init),
        grid=(b,),
        in_specs=[seq(nq), seq(nk, k_off // nk, **once), seq(nk, k_off // nk, **once),
                  pl.BlockSpec((4, A_QKDIM), lambda bi: (0, 0)),
                  pl.BlockSpec((1, GROUP_W), lambda bi: (0, 0))],
        out_specs=seq(nq),
        out_shape=jax.ShapeDtypeStruct((b, nq, GROUP_W), BF16),
        scratch_shapes=[pltpu.VMEM((rows, GROUP_W), BF16), s_shape, s_shape,
                        pltpu.VMEM((rows, LANES), F32), pltpu.VMEM((rows, LANES), F32),
                        pltpu.VMEM((rows, LANES), F32), pltpu.VMEM((rows, LANES), F32),
                        pltpu.VMEM((rows, GROUP_W), F32), pltpu.VMEM((rows, GROUP_W), F32)],
        compiler_params=_cparams("arbitrary"),
        name="diff_attention",
    )(q, k, v, lam_p, jnp.tile(subln_g, A_HEADS).reshape(1, GROUP_W))


def _ret_kernel(q_ref, k_ref, v_ref, gate_ref, dec_ref, s0_ref, y_ref, st_ref, o_scr, s_scr, intra_scr, dec_scr):
    nb, n = q_ref.shape[0], q_ref.shape[1]
    c = RET_CHUNK
    nch = n // c
    lane = lax.broadcasted_iota(jnp.int32, (1, GROUP_W), 1)
    head_masks = [(lane >= h * B_DIM) & (lane < (h + 1) * B_DIM) for h in range(B_HEADS)]
    rr = lax.broadcasted_iota(jnp.int32, (GROUP_W, GROUP_W), 0) // B_DIM
    cc = lax.broadcasted_iota(jnp.int32, (GROUP_W, GROUP_W), 1) // B_DIM
    block_diag = rr == cc
    pos = lax.broadcasted_iota(jnp.int32, (c, 1), 0).astype(F32)
    ii = lax.broadcasted_iota(jnp.int32, (c, c), 0)
    jj = lax.broadcasted_iota(jnp.int32, (c, c), 1)

    c_decs = []
    for d in range(2):
        lg = -jnp.exp(dec_ref[d:d + 1, :])
        rel = (ii - jj) if d == 0 else (jj - ii)
        relf = jnp.maximum(rel, 0).astype(F32)
        for h in range(B_HEADS):
            intra_scr[d, h * c:(h + 1) * c, :] = jnp.where(
                rel >= 0, jnp.exp(relf * lg[:, h * B_DIM:h * B_DIM + 1]), 0.0)
        if d == 0:
            dec_scr[d, 0] = jnp.exp((pos + 1.0) * lg)
            dec_scr[d, 1] = jnp.exp((c - 1.0 - pos) * lg)
        else:
            dec_scr[d, 0] = jnp.exp((c - pos) * lg)
            dec_scr[d, 1] = jnp.exp(pos * lg)
        c_decs.append(jnp.exp(float(c) * lg))
    s_scr[...] = s0_ref[...]

    def chunk(i, carry):
        for bi in range(nb):
            for d in range(2):
                ci = i if d == 0 else nch - 1 - i
                r = pl.multiple_of(ci * c, c)
                qc = q_ref[bi, pl.ds(r, c), :]
                kc = k_ref[bi, pl.ds(r, c), :]
                vc = v_ref[bi, pl.ds(r, c), :]
                qb = qc.astype(BF16)
                q4 = jnp.concatenate([jnp.where(mk, qb, jnp.zeros_like(qb)) for mk in head_masks], axis=0)
                sc = (_dot_nt(q4, kc.astype(BF16)) * intra_scr[d]).astype(BF16)
                state = s_scr[bi, d]
                out = _dot((qc * dec_scr[d, 0]).astype(BF16), state.astype(BF16))
                for h in range(B_HEADS):
                    out = out + jnp.where(head_masks[h], _dot(sc[h * c:(h + 1) * c], vc), 0.0)
                o_scr[bi, d, pl.ds(r, c), :] = out
                kd_t = (kc * dec_scr[d, 1]).T.astype(BF16)
                s_scr[bi, d] = c_decs[d] * state + jnp.where(block_diag, _dot(kd_t, vc), 0.0)
        return carry

    lax.fori_loop(0, nch, chunk, 0)
    st_ref[...] = s_scr[...]

    def finish(i, carry):
        r = pl.multiple_of(i * c, c)
        for bi in range(nb):
            o = o_scr[bi, 0, pl.ds(r, c), :] + o_scr[bi, 1, pl.ds(r, c), :]
            y = _silu(gate_ref[bi, pl.ds(r, c), :]) * (o * lax.rsqrt(_head_mean(o * o, B_DIM) + EPS))
            y_ref[bi, pl.ds(r, c), :] = y.astype(BF16)
        return carry

    lax.fori_loop(0, nch, finish, 0)


def _retention(q, k, v, gate, dec_lanes, s0):
    b, n, _ = q.shape
    nb = RET_BATCHES if b % RET_BATCHES == 0 else 1
    seq = pl.BlockSpec((nb, n, GROUP_W), lambda bi: (bi, 0, 0))
    st = pl.BlockSpec((nb, 2, GROUP_W, GROUP_W), lambda bi: (bi, 0, 0, 0))
    return pl.pallas_call(
        _ret_kernel,
        grid=(b // nb,),
        in_specs=[seq, seq, seq, seq, pl.BlockSpec((2, GROUP_W), lambda bi: (0, 0)), st],
        out_specs=[seq, st],
        out_shape=[jax.ShapeDtypeStruct((b, n, GROUP_W), BF16),
                   jax.ShapeDtypeStruct((b, 2, GROUP_W, GROUP_W), F32)],
        scratch_shapes=[pltpu.VMEM((nb, 2, n, GROUP_W), F32), pltpu.VMEM((nb, 2, GROUP_W, GROUP_W), F32),
                        pltpu.VMEM((2, B_HEADS * RET_CHUNK, RET_CHUNK), F32),
                        pltpu.VMEM((2, 2, RET_CHUNK, GROUP_W), F32)],
        compiler_params=_cparams("arbitrary"),
        name="retention",
    )(q, k, v, gate, dec_lanes, s0)


def _convpool_kernel(uc_ref, ud_ref, cw_ref, cb_ref, lg_ref, lb_ref, pw_ref, ps_ref, yc_ref, yd_ref,
                     ubuf, dbuf, shift_scr):
    n = uc_ref.shape[1]
    rb = CONV_ROW_BLOCK
    for buf in (ubuf, dbuf):
        buf[0:PAD_ROWS, :] = jnp.zeros((PAD_ROWS, GROUP_W), F32)
        buf[PAD_ROWS + n:, :] = jnp.zeros((TAIL_ROWS, GROUP_W), F32)

    def fill(i, carry):
        r = pl.multiple_of(i * RET_CHUNK, RET_CHUNK)
        u2 = uc_ref[0, pl.ds(r, RET_CHUNK), :]
        ubuf[pl.ds(PAD_ROWS + r, RET_CHUNK), :] = u2[:, :GROUP_W] * _sigmoid(u2[:, GROUP_W:])
        dbuf[pl.ds(PAD_ROWS + r, RET_CHUNK), :] = ud_ref[0, pl.ds(r, RET_CHUNK), :]
        return carry

    lax.fori_loop(0, n // RET_CHUNK, fill, 0)

    lane = lax.broadcasted_iota(jnp.int32, (1, GROUP_W), 1)
    half = jnp.left_shift(1, lane // POOL_GROUP)
    first_tap = PAD_ROWS - C_KSIZE // 2

    def block(i, carry):
        r = pl.multiple_of(i * rb, rb)
        uwin = ubuf[pl.ds(r, rb + 2 * PAD_ROWS), :]
        dwin = dbuf[pl.ds(r, rb + PAD_ROWS + TAIL_ROWS), :]
        acc = jnp.zeros((rb, GROUP_W), F32) + cb_ref[...]
        for ph in range(SUBLANES):
            shift_scr[ph] = uwin[ph:ph + rb + 2 * PAD_ROWS - SUBLANES]
            for a8 in range(0, 2 * PAD_ROWS, SUBLANES):
                kk = a8 + ph - first_tap
                if 0 <= kk < C_KSIZE:
                    acc = acc + shift_scr[ph, a8:a8 + rb, :] * cw_ref[kk:kk + 1, :]
        mu = jnp.mean(acc, axis=-1, keepdims=True)
        var = jnp.mean(jnp.square(acc - mu), axis=-1, keepdims=True)
        yn = (acc - mu) * lax.rsqrt(var + EPS) * lg_ref[...] + lb_ref[...]
        yc_ref[0, pl.ds(r, rb), :] = _silu(yn).astype(BF16)
        d0 = dwin[:, :LANES]
        d1 = dwin[:, LANES:]
        p = PAD_ROWS
        s2 = d0[p - 1:p - 1 + rb] + d0[p:p + rb]
        s4 = s2 + d0[p - 2:p - 2 + rb] + d0[p + 1:p + 1 + rb]
        q2 = d1[0:rb + 32] + d1[1:rb + 33]
        q4 = q2[0:rb + 24] + q2[2:rb + 26]
        q8 = q4[0:rb + 16] + q4[4:rb + 20]
        s8 = q8[p - 4:p - 4 + rb]
        s16 = q8[p - 8:p - 8 + rb] + q8[p:p + rb]
        lane1 = lane[:, :LANES]
        tot = jnp.concatenate([jnp.where(lane1 < POOL_GROUP, s2, s4), jnp.where(lane1 < POOL_GROUP, s8, s16)],
                              axis=1)
        t = r + lax.broadcasted_iota(jnp.int32, (rb, 1), 0)
        cnt = (jnp.minimum(t + half, n) - jnp.maximum(t - half, 0)).astype(F32)
        pooled = tot / cnt - dwin[p:p + rb]
        yd_ref[0, pl.ds(r, rb), :] = (_dot(pooled.astype(BF16), pw_ref[...]) * ps_ref[...]).astype(BF16)
        return carry

    lax.fori_loop(0, n // rb, block, 0)


def _convpool(uc, ud, cw, cb, lng, lnb, pw_bd, pscale):
    b, n, _ = ud.shape
    seq = lambda w: pl.BlockSpec((1, n, w), lambda bi: (bi, 0, 0))
    row = pl.BlockSpec((1, GROUP_W), lambda bi: (0, 0))
    return pl.pallas_call(
        _convpool_kernel,
        grid=(b,),
        in_specs=[seq(2 * GROUP_W), seq(GROUP_W),
                  pl.BlockSpec((C_KSIZE, GROUP_W), lambda bi: (0, 0)), row, row, row,
                  pl.BlockSpec((GROUP_W, GROUP_W), lambda bi: (0, 0)), row],
        out_specs=[seq(GROUP_W), seq(GROUP_W)],
        out_shape=[jax.ShapeDtypeStruct((b, n, GROUP_W), BF16)] * 2,
        scratch_shapes=[pltpu.VMEM((n + PAD_ROWS + TAIL_ROWS, GROUP_W), F32)] * 2
                       + [pltpu.VMEM((SUBLANES, CONV_ROW_BLOCK + 2 * PAD_ROWS - SUBLANES, GROUP_W), F32)],
        compiler_params=_cparams("arbitrary"),
        name="conv_pool",
    )(uc, ud, cw, cb, lng, lnb, pw_bd, pscale)


def _tail_kernel(per_seq, final, *refs):
    y_refs = refs[0:12]
    x_ref, xp_ref, xn_ref = refs[12:15]
    wo_ref, wu_ref, dw_ref, db_ref, wd_ref, mod_ref, g_ref, fg_ref, o_ref = refs[15:24]
    ybuf, xbuf, hbuf, ubuf_a, ubuf_b = refs[24:]
    acc = o_ref
    tm = x_ref.shape[0]
    hr = BF16_ROWS
    i = pl.program_id(0)
    first = (i % per_seq) == 0
    last = (i % per_seq) == per_seq - 1

    for gi in range(N_MIXERS):
        y_ref, yp_ref, yn_ref = y_refs[3 * gi:3 * gi + 3]
        ybuf[gi, 0:hr, :] = yp_ref[...]
        ybuf[gi, hr:hr + tm, :] = y_ref[...]
        ybuf[gi, hr + tm:, :] = yn_ref[...]
    xbuf[0:hr, :] = xp_ref[...]
    xbuf[hr:hr + tm, :] = x_ref[...]
    xbuf[hr + tm:, :] = xn_ref[...]
    y = jnp.zeros(xbuf.shape, F32)
    for gi in range(N_MIXERS):
        y = y + _dot(ybuf[gi], wo_ref[0, gi * GROUP_W:(gi + 1) * GROUP_W, :])
    xbuf[...] = xbuf[...] + mod_ref[0, 2:3, :] * y
    h = _norm_mod(xbuf[...], g_ref[...], mod_ref[0, 3:4, :], mod_ref[0, 4:5, :]).astype(BF16)
    row = lax.broadcasted_iota(jnp.int32, (tm + 2 * hr, 1), 0)
    outside = (first & (row < hr)) | (last & (row >= hr + tm))
    hbuf[...] = jnp.where(outside, jnp.zeros_like(h), h)

    acc[...] = jnp.zeros_like(acc)
    tf = FFN_CHUNK
    nj = D_FF // tf

    def cols(j, half):
        return pl.ds(pl.multiple_of(half * D_FF + j * tf, LANES), tf)

    def up(j, buf):
        buf[:, 0:tf] = _dot(hbuf[...], wu_ref[0, :, cols(j, 0)])
        buf[:, tf:] = _dot(hbuf[...], wu_ref[0, :, cols(j, 1)])

    def mix(j, buf):
        w = jnp.concatenate([dw_ref[:, cols(j, 0)], dw_ref[:, cols(j, 1)]], axis=1)
        u = jnp.concatenate([db_ref[:, cols(j, 0)], db_ref[:, cols(j, 1)]], axis=1)
        for kk in range(FFN_KSIZE):
            u = u + buf[pl.ds(hr - FFN_KSIZE // 2 + kk, tm), :] * w[kk:kk + 1, :]
        act = (u[:, :tf] * _silu(u[:, tf:])).astype(BF16)
        acc[...] = acc[...] + _dot(act, wd_ref[0, pl.ds(pl.multiple_of(j * tf, tf), tf), :])

    up(0, ubuf_a)

    def pair(p, carry):
        up(2 * p + 1, ubuf_b)
        mix(2 * p, ubuf_a)
        up(2 * p + 2, ubuf_a)
        mix(2 * p + 1, ubuf_b)
        return carry

    pair(0, 0)
    lax.fori_loop(1, (nj - 1) // 2 - 1, pair, 0)
    pair((nj - 1) // 2 - 1, 0)
    mix(nj - 1, ubuf_a)
    x = xbuf[hr:hr + tm, :] + mod_ref[0, 5:6, :] * acc[...]
    if final:
        ms = jnp.mean(x * x, axis=-1, keepdims=True)
        x = x * lax.rsqrt(ms + EPS) * fg_ref[...]
    o_ref[...] = x


def _tail(ys, w_out_bf, wu_r, dw_r, db_r, wd_r, layer, x2, mod, g2, final_g, seq, final):
    t = x2.shape[0]
    tm = min(TOKEN_TILE, seq)
    per_seq = seq // tm
    nblk = tm // BF16_ROWS
    nj = D_FF // FFN_CHUNK
    assert nj % 2 == 1
    tok = lambda i: (i, 0)
    prev = lambda i: (jnp.maximum(i * nblk - 1, 0), 0)
    nxt = lambda i: (jnp.minimum((i + 1) * nblk, t // BF16_ROWS - 1), 0)
    const = lambda shape: pl.BlockSpec(shape, lambda i: (0,) * len(shape), pipeline_mode=pl.Buffered(1))
    stacked = lambda shape: pl.BlockSpec((1,) + shape, lambda i: (layer,) + (0,) * len(shape),
                                         pipeline_mode=pl.Buffered(1))
    trio = lambda w: [pl.BlockSpec((tm, w), tok), pl.BlockSpec((BF16_ROWS, w), prev), pl.BlockSpec((BF16_ROWS, w), nxt)]
    halo_rows = tm + 2 * BF16_ROWS
    args = []
    for y in ys:
        args += [y, y, y]
    return pl.pallas_call(
        functools.partial(_tail_kernel, per_seq, final),
        grid=(t // tm,),
        in_specs=trio(GROUP_W) * N_MIXERS + trio(D_MODEL)
                 + [stacked((D_MODEL, D_MODEL)),
                    stacked((D_MODEL, 2 * D_FF)), const((FFN_KSIZE, 2 * D_FF)),
                    const((1, 2 * D_FF)), stacked((D_FF, D_MODEL)),
                    pl.BlockSpec((1, N_MOD, D_MODEL), lambda i: (i // per_seq, 0, 0)),
                    const((1, D_MODEL)), const((1, D_MODEL))],
        out_specs=pl.BlockSpec((tm, D_MODEL), tok),
        out_shape=jax.ShapeDtypeStruct((t, D_MODEL), F32),
        scratch_shapes=[pltpu.VMEM((N_MIXERS, halo_rows, GROUP_W), BF16),
                        pltpu.VMEM((halo_rows, D_MODEL), F32),
                        pltpu.VMEM((halo_rows, D_MODEL), BF16),
                        pltpu.VMEM((halo_rows, 2 * FFN_CHUNK), F32),
                        pltpu.VMEM((halo_rows, 2 * FFN_CHUNK), F32)],
        compiler_params=_cparams("arbitrary"),
        name="out_ffn",
    )(*args, x2, x2, x2, w_out_bf, wu_r, dw_r, db_r, wd_r, mod, g2, final_g)


def kernel(x, c, ctx, c_ctx, w_mod, b_mod, norm1_g, norm2_g, w_in, w_out, diff_lambda, diff_subln_g, ret_decay,
           conv_dw_w, conv_dw_b, conv_ln_g, conv_ln_b, pool_w, pool_scale, ffn_w_up, ffn_dw_w, ffn_dw_b,
           ffn_w_down, final_g):
    b, n, d = x.shape
    nc = ctx.shape[1]
    depth = w_mod.shape[0]
    assert d == D_MODEL and b + 1 <= MOD_ROWS
    assert n % RET_CHUNK == 0 and nc % RET_CHUNK == 0 and n % min(TOKEN_TILE, n) == 0

    cond = jnp.zeros((MOD_ROWS, D_MODEL), F32).at[:b].set(c).at[b].set(c_ctx)
    mods = _modulation(cond, w_mod, b_mod)

    tab_x = _rope_tables(n, A_QKDIM // 4) + _rope_tables(n, B_DIM // 4)
    one, zero = jnp.ones((nc, LANES), F32), jnp.zeros((nc, LANES), F32)
    tab_c = (one, zero, one, zero)

    x2 = x.reshape(b * n, d)
    c2 = ctx.reshape(b * nc, d)
    row = lambda v: v.reshape(1, -1)
    g3 = lambda a, s: a.reshape(b, s, GROUP_W)
    zero_state = jnp.zeros((b, 2, GROUP_W, GROUP_W), F32)

    w_in_bf, w_out_bf = w_in.astype(BF16), w_out.astype(BF16)
    wu_bf, wd_bf = ffn_w_up.astype(BF16), ffn_w_down.astype(BF16)

    for l in range(depth):
        need_ctx = l < depth - 1
        lambda_init = 0.8 - 0.6 * math.exp(-0.3 * l)
        mod_x = mods[l, :b].reshape(b, N_MOD, D_MODEL)
        mod_c = jnp.broadcast_to(mods[l, b].reshape(1, N_MOD, D_MODEL), (b, N_MOD, D_MODEL))
        dw_r = ffn_dw_w[l]
        db_r = ffn_dw_b[l].reshape(1, -1)
        dec_lanes = jnp.repeat(ret_decay[l], B_DIM, axis=-1)
        pw_bd = jnp.zeros((GROUP_W, GROUP_W), F32)
        for gi in range(len(POOL_WINDOWS)):
            pw_bd = pw_bd.at[gi * POOL_GROUP:(gi + 1) * POOL_GROUP, gi * POOL_GROUP:(gi + 1) * POOL_GROUP].set(
                pool_w[l, gi])
        pw_bd = pw_bd.astype(BF16)
        mixer_params = (conv_dw_w[l], row(conv_dw_b[l]), row(conv_ln_g[l]), row(conv_ln_b[l]), pw_bd,
                        row(pool_scale[l]))

        pc = _in_proj(c2, mod_c, row(norm1_g[l]), w_in_bf, l, tab_c, nc, n + nc, n)
        aq_c, ak_part, av_part, bq_c, bk_c, bv_c, bg_c, uc_c, ud_c = pc
        px = _in_proj(x2, mod_x, row(norm1_g[l]), w_in_bf, l, tab_x, n, n + nc, 0, kv_bufs=(ak_part, av_part))
        aq_x, ak_all, av_all, bq_x, bk_x, bv_x, bg_x, uc_x, ud_x = px

        ya_x = _attention(g3(aq_x, n), ak_all, av_all, diff_lambda[l], diff_subln_g[l], lambda_init)
        yb_c, st_c = _retention(g3(bq_c, nc), g3(bk_c, nc), g3(bv_c, nc), g3(bg_c, nc), dec_lanes, zero_state)
        yb_x, _ = _retention(g3(bq_x, n), g3(bk_x, n), g3(bv_x, n), g3(bg_x, n), dec_lanes, st_c)
        yc_x, yd_x = _convpool(uc_x.reshape(b, n, 2 * GROUP_W), g3(ud_x, n), *mixer_params)
        flat = lambda a: a.reshape(-1, GROUP_W)
        x2 = _tail([flat(ya_x), flat(yb_x), flat(yc_x), flat(yd_x)], w_out_bf, wu_bf, dw_r, db_r, wd_bf, l, x2,
                   mod_x, row(norm2_g[l]), row(final_g), n, final=not need_ctx)

        if need_ctx:
            ya_c = _attention(g3(aq_c, nc), ak_all, av_all, diff_lambda[l], diff_subln_g[l], lambda_init,
                              key_rows=(n, nc))
            yc_c, yd_c = _convpool(uc_c.reshape(b, nc, 2 * GROUP_W), g3(ud_c, nc), *mixer_params)
            c2 = _tail([flat(ya_c), flat(yb_c), flat(yc_c), flat(yd_c)], w_out_bf, wu_bf, dw_r, db_r, wd_bf, l, c2,
                       mod_c, row(norm2_g[l]), row(final_g), nc, final=False)

    return x2.reshape(b, n, d)
```

```python
import functools
import math

import numpy as np
import jax
import jax.numpy as jnp
from jax import lax
from jax.experimental import pallas as pl
from jax.experimental.pallas import tpu as pltpu

F32 = jnp.float32
BF16 = jnp.bfloat16

D_MODEL = 1024
DEPTH = 4
GRID_W = 64
N_MIXERS = 4
GROUP_W = D_MODEL // N_MIXERS
A_HEADS = 4
A_VDIM = GROUP_W // A_HEADS
A_QKDIM = A_VDIM // 2
B_HEADS = 4
B_DIM = GROUP_W // B_HEADS
C_KSIZE = 31
POOL_WINDOWS = (2, 4, 8, 16)
POOL_GROUP = GROUP_W // len(POOL_WINDOWS)
D_FF = 2816
FFN_KSIZE = 3
RET_CHUNK = 256
ROPE_BASE = 10000.0
EPS = 1e-6
N_MOD = 6
OFF_AQ = 0
OFF_BQ = OFF_AQ + GROUP_W
OFF_BG = OFF_BQ + GROUP_W
OFF_C = OFF_BG + GROUP_W
OFF_D = OFF_C + 2 * GROUP_W
OFF_KV = OFF_D + GROUP_W
KV_AK = 0
KV_AV = GROUP_W
KV_BK = 2 * GROUP_W
KV_BV = 3 * GROUP_W
D_IN = OFF_KV + 4 * GROUP_W

LANES = 128
BF16_ROWS = 16
MOD_ROWS = 16
TOKEN_TILE = 1024
ATTN_Q_TILE = 256
ATTN_KEY_TILE = 256
ATTN_TILES_PER_ITER = 3
LOG2E = 1.4426950408889634
CONV_ROW_BLOCK = 256
SUBLANES = 8
PAD_ROWS = 16
TAIL_ROWS = 32
RET_BATCHES = 2
FFN_CHUNK = 256
VMEM_LIMIT = 58 * 1024 * 1024


def _cparams(*sem):
    return pltpu.CompilerParams(dimension_semantics=sem, vmem_limit_bytes=VMEM_LIMIT)


def _sigmoid(x):
    return 1.0 / (1.0 + jnp.exp(-x))


def _silu(x):
    return x * _sigmoid(x)


def _dot(a, b):
    return jnp.dot(a, b, preferred_element_type=F32)


def _dot_nt(a, b):
    return lax.dot_general(a, b, (((1,), (1,)), ((), ())), preferred_element_type=F32)


def _rope_tables(n, blk):
    lane = np.arange(LANES)
    freq = lane % blk
    is_a = (lane % (2 * blk)) < blk
    is_row = (lane % (4 * blk)) < 2 * blk
    inv = ROPE_BASE ** (-freq.astype(np.float64) / blk)
    t = np.arange(n)
    pos = np.where(is_row[None, :], (t // GRID_W)[:, None], (t % GRID_W)[:, None]).astype(np.float64)
    ang = pos * inv[None, :]
    cos = np.cos(ang)
    sin = np.sin(ang) * np.where(is_a, -1.0, 1.0)[None, :]
    return jnp.asarray(cos, F32), jnp.asarray(sin, F32)


def _rope(x, cos, sin, blk):
    lane = lax.broadcasted_iota(jnp.int32, (1, LANES), 1)
    is_a = (lane % (2 * blk)) < blk
    outs = []
    for j in range(GROUP_W // LANES):
        xh = x[:, j * LANES:(j + 1) * LANES]
        from_hi = pltpu.roll(xh, LANES - blk, axis=1)
        from_lo = pltpu.roll(xh, blk, axis=1)
        outs.append(xh * cos + jnp.where(is_a, from_hi, from_lo) * sin)
    return jnp.concatenate(outs, axis=1)


def _head_mean(a, width):
    r = lax.broadcasted_iota(jnp.int32, (GROUP_W, GROUP_W), 0) // width
    c = lax.broadcasted_iota(jnp.int32, (GROUP_W, GROUP_W), 1) // width
    seg = jnp.where(r == c, 1.0 / width, 0.0).astype(BF16)
    a1 = a.astype(BF16)
    r1 = a - a1.astype(F32)
    a2 = r1.astype(BF16)
    a3 = (r1 - a2.astype(F32)).astype(BF16)
    return _dot(a1, seg) + _dot(a2, seg) + _dot(a3, seg)


def _mod_kernel(s_ref, w_ref, b_ref, o_ref):
    s = _silu(s_ref[...])
    o_ref[0] = _dot(s.astype(BF16), w_ref[0].astype(BF16)) + b_ref[0]


def _modulation(cond, w_mod, b_mod):
    depth = w_mod.shape[0]
    return pl.pallas_call(
        _mod_kernel,
        grid=(depth, N_MOD),
        in_specs=[
            pl.BlockSpec((MOD_ROWS, D_MODEL), lambda l, j: (0, 0)),
            pl.BlockSpec((1, D_MODEL, D_MODEL), lambda l, j: (l, 0, j)),
            pl.BlockSpec((1, 1, D_MODEL), lambda l, j: (l, 0, j)),
        ],
        out_specs=pl.BlockSpec((1, MOD_ROWS, D_MODEL), lambda l, j: (l, 0, j)),
        out_shape=jax.ShapeDtypeStruct((depth, MOD_ROWS, N_MOD * D_MODEL), F32),
        compiler_params=_cparams("arbitrary", "arbitrary"),
        name="modulation",
    )(cond, w_mod, b_mod.reshape(depth, 1, N_MOD * D_MODEL))


def _norm_mod(x, g, shift, scale):
    ms = jnp.mean(x * x, axis=-1, keepdims=True)
    return (x * lax.rsqrt(ms + EPS) * g) * (1.0 + scale) + shift


def _in_kernel(kv_rows, x_ref, mod_ref, g_ref, w_ref, ca_ref, sa_ref, cb_ref, sb_ref, *refs):
    aq_ref, ak_ref, av_ref, bq_ref, bk_ref, bv_ref, bg_ref, uc_ref, ud_ref = refs[-9:]
    h = _norm_mod(x_ref[...], g_ref[...], mod_ref[0, 0:1, :], mod_ref[0, 1:2, :]).astype(BF16)

    def proj(off, width=GROUP_W):
        return _dot(h, w_ref[0, :, off:off + width])

    ca, sa, cb, sb = ca_ref[...], sa_ref[...], cb_ref[...], sb_ref[...]
    aq_ref[...] = (_rope(proj(OFF_AQ), ca, sa, A_QKDIM // 4) * (A_QKDIM ** -0.5 * LOG2E)).astype(BF16)
    ak = _rope(proj(OFF_KV + KV_AK), ca, sa, A_QKDIM // 4).astype(BF16)
    av = proj(OFF_KV + KV_AV).astype(BF16)
    if kv_rows is None:
        ak_ref[0] = ak
        av_ref[0] = av
    else:
        off, cnt = kv_rows
        for ref, val in ((ak_ref, ak), (av_ref, av)):
            ref[0] = jnp.zeros(ref.shape[1:], BF16)
            ref[0, off:off + cnt, :] = val
    bq_ref[...] = _rope(proj(OFF_BQ), cb, sb, B_DIM // 4)
    bk_ref[...] = _rope(proj(OFF_KV + KV_BK), cb, sb, B_DIM // 4) * (B_DIM ** -0.5)
    bv_ref[...] = proj(OFF_KV + KV_BV).astype(BF16)
    bg_ref[...] = proj(OFF_BG)
    uc_ref[...] = proj(OFF_C, 2 * GROUP_W)
    ud_ref[...] = proj(OFF_D)


def _in_proj(x2, mod, g, w_bf, layer, tables, seq, kv_rows, kv_offset, kv_bufs=None):
    t = x2.shape[0]
    b = t // seq
    tm = min(TOKEN_TILE, seq)
    per_seq = seq // tm
    assert kv_offset % tm == 0
    tok = lambda i: (i, 0)
    tab = pl.BlockSpec((tm, LANES), lambda i: (i % per_seq, 0))
    gw = lambda dt: jax.ShapeDtypeStruct((t, GROUP_W), dt)
    kv = jax.ShapeDtypeStruct((b, kv_rows, GROUP_W), BF16)
    gspec = pl.BlockSpec((tm, GROUP_W), tok)
    if kv_bufs is None:
        assert per_seq == 1
        kv_rows_arg = (kv_offset, tm)
        kvspec = pl.BlockSpec((1, kv_rows, GROUP_W), lambda i: (i, 0, 0))
    else:
        kv_rows_arg = None
        kvspec = pl.BlockSpec((1, tm, GROUP_W), lambda i: (i // per_seq, i % per_seq + kv_offset // tm, 0))
    in_specs = [
        pl.BlockSpec((tm, D_MODEL), tok),
        pl.BlockSpec((1, N_MOD, D_MODEL), lambda i: (i // per_seq, 0, 0)),
        pl.BlockSpec((1, D_MODEL), lambda i: (0, 0)),
        pl.BlockSpec((1, D_MODEL, D_IN), lambda i: (layer, 0, 0)),
        tab, tab, tab, tab,
    ]
    args = [x2, mod, g, w_bf, *tables]
    aliases = {}
    if kv_bufs is not None:
        aliases = {len(args): 1, len(args) + 1: 2}
        in_specs += [pl.BlockSpec(memory_space=pl.ANY)] * 2
        args += list(kv_bufs)
    return pl.pallas_call(
        functools.partial(_in_kernel, kv_rows_arg),
        grid=(t // tm,),
        in_specs=in_specs,
        out_specs=[gspec, kvspec, kvspec] + [gspec] * 4 + [pl.BlockSpec((tm, 2 * GROUP_W), tok), gspec],
        out_shape=[gw(BF16), kv, kv, gw(F32), gw(F32), gw(BF16), gw(F32),
                   jax.ShapeDtypeStruct((t, 2 * GROUP_W), F32), gw(F32)],
        input_output_aliases=aliases,
        compiler_params=_cparams("arbitrary"),
        name="in_proj",
    )(*args)


def _attn_kernel(lambda_init, q_ref, k_ref, v_ref, lam_ref, g_ref, o_ref,
                 qs_scr, s_a, s_b, mx_a, mx_b, l_a, l_b, acc_a, acc_b):
    nq = q_ref.shape[1]
    tq = min(ATTN_Q_TILE, nq)
    n_qt = nq // tq
    n_maps = 2 * A_HEADS
    kt = ATTN_KEY_TILE
    n_tiles = k_ref.shape[1] // kt
    per_iter = ATTN_TILES_PER_ITER if n_tiles % ATTN_TILES_PER_ITER == 0 else 1
    n_iter = n_tiles // per_iter
    lp = lam_ref[...]
    lam = (jnp.exp(jnp.sum(lp[0:1] * lp[1:2], axis=-1, keepdims=True))
           - jnp.exp(jnp.sum(lp[2:3] * lp[3:4], axis=-1, keepdims=True)) + lambda_init)
    lane = lax.broadcasted_iota(jnp.int32, (1, GROUP_W), 1)

    def q_rows(t):
        return pl.ds(pl.multiple_of(t * tq, tq), tq)

    def stack_queries(t):
        q = q_ref[0, q_rows(t), :]
        for u in range(n_maps):
            keep = (lane >= u * A_QKDIM) & (lane < (u + 1) * A_QKDIM)
            qs_scr[u * tq:(u + 1) * tq, :] = jnp.where(keep, q, jnp.zeros_like(q))

    def scores(j, s_scr, mx_scr, first):
        for t in range(per_iter):
            tile = j * per_iter + t
            r = tile * kt if isinstance(tile, int) else pl.multiple_of(tile * kt, kt)
            s = _dot_nt(qs_scr[...], k_ref[0, pl.ds(r, kt), :])
            s_scr[tile] = s
            m = jnp.maximum(s[:, :LANES], s[:, LANES:])
            mx_scr[...] = m if first and t == 0 else jnp.maximum(mx_scr[...], m)

    def values(j, s_scr, mx_scr, l_scr, acc_scr, first):
        for t in range(per_iter):
            tile = j * per_iter + t
            r = tile * kt if isinstance(tile, int) else pl.multiple_of(tile * kt, kt)
            m = mx_scr[...]
            e = jnp.exp2(s_scr[tile] - jnp.concatenate([m, m], axis=1))
            pv = _dot(e.astype(BF16), v_ref[0, pl.ds(r, kt), :])
            if first and t == 0:
                l_scr[...] = e[:, :LANES] + e[:, LANES:]
                acc_scr[...] = pv
            else:
                l_scr[...] = l_scr[...] + (e[:, :LANES] + e[:, LANES:])
                acc_scr[...] = acc_scr[...] + pv

    def end_scores(mx_scr):
        mx_scr[...] = jnp.broadcast_to(jnp.max(mx_scr[...], axis=-1, keepdims=True), mx_scr.shape)

    def end_values(t, l_scr, acc_scr):
        o = jnp.zeros((tq, GROUP_W), F32)
        for hd in range(A_HEADS):
            parts = []
            for m in range(2):
                u = 2 * hd + m
                rows = slice(u * tq, (u + 1) * tq)
                parts.append(acc_scr[rows, :] * (1.0 / jnp.sum(l_scr[rows, :], axis=-1, keepdims=True)))
            o = jnp.where((lane >= hd * A_VDIM) & (lane < (hd + 1) * A_VDIM), parts[0] - lam * parts[1], o)
        y = o * lax.rsqrt(_head_mean(o * o, A_VDIM) + EPS) * g_ref[...] * (1.0 - lambda_init)
        o_ref[0, q_rows(t), :] = y.astype(BF16)

    bufs = ((s_a, mx_a, l_a, acc_a), (s_b, mx_b, l_b, acc_b))

    def phase(t, parity, has_prev, has_next):
        s_cur, mx_cur, l_cur, acc_cur = bufs[parity]
        s_nxt, mx_nxt, l_prev, acc_prev = bufs[1 - parity]
        end_scores(mx_cur)
        if has_next:
            stack_queries(t + 1)
            scores(0, s_nxt, mx_nxt, True)
        values(0, s_cur, mx_cur, l_cur, acc_cur, True)
        if has_prev:
            end_values(t - 1, l_prev, acc_prev)

        def body(j, carry):
            if has_next:
                scores(j, s_nxt, mx_nxt, False)
            values(j, s_cur, mx_cur, l_cur, acc_cur, False)
            return carry

        lax.fori_loop(1, n_iter, body, 0)

    assert n_qt == 1 or n_qt % 2 == 0
    stack_queries(0)
    scores(0, s_a, mx_a, True)
    lax.fori_loop(1, n_iter, lambda j, c: (scores(j, s_a, mx_a, False), c)[1], 0)
    if n_qt == 1:
        phase(0, 0, False, False)
        end_values(0, l_a, acc_a)
    else:
        phase(0, 0, False, True)

        def pair(p, carry):
            phase(2 * p + 1, 1, True, True)
            phase(2 * p + 2, 0, True, True)
            return carry

        lax.fori_loop(0, (n_qt - 2) // 2, pair, 0)
        phase(n_qt - 1, 1, True, False)
        end_values(n_qt - 1, l_b, acc_b)


def _attention(q, k, v, lam_p, subln_g, lambda_init, key_rows=None):
    b, nq, _ = q.shape
    k_off, nk = (0, k.shape[1]) if key_rows is None else key_rows
    assert k_off % nk == 0
    rows = 2 * A_HEADS * min(ATTN_Q_TILE, nq)
    seq = lambda n, blk=0, **kw: pl.BlockSpec((1, n, GROUP_W), lambda bi: (bi, blk, 0), **kw)
    once = dict(pipeline_mode=pl.Buffered(1))
    s_shape = pltpu.VMEM((nk // ATTN_KEY_TILE, rows, ATTN_KEY_TILE), F32)
    return pl.pallas_call(
        functools.partial(_attn_kernel, lambda_init),
        grid=(b,),
        in_specs=[seq(nq), seq(nk, k_off // nk, **once), seq(nk, k_off // nk, **once),
                  pl.BlockSpec((4, A_QKDIM), lambda bi: (0, 0)),
                  pl.BlockSpec((1, GROUP_W), lambda bi: (0, 0))],
        out_specs=seq(nq),
        out_shape=jax.ShapeDtypeStruct((b, nq, GROUP_W), BF16),
        scratch_shapes=[pltpu.VMEM((rows, GROUP_W), BF16), s_shape, s_shape,
                        pltpu.VMEM((rows, LANES), F32), pltpu.VMEM((rows, LANES), F32),
                        pltpu.VMEM((rows, LANES), F32), pltpu.VMEM((rows, LANES), F32),
                        pltpu.VMEM((rows, GROUP_W), F32), pltpu.VMEM((rows, GROUP_W), F32)],
        compiler_params=_cparams("arbitrary"),
        name="diff_attention",
    )(q, k, v, lam_p, jnp.tile(subln_g, A_HEADS).reshape(1, GROUP_W))


def _ret_kernel(q_ref, k_ref, v_ref, gate_ref, dec_ref, s0_ref, y_ref, st_ref, o_scr, s_scr, intra_scr, dec_scr):
    nb, n = q_ref.shape[0], q_ref.shape[1]
    c = RET_CHUNK
    nch = n // c
    lane = lax.broadcasted_iota(jnp.int32, (1, GROUP_W), 1)
    head_masks = [(lane >= h * B_DIM) & (lane < (h + 1) * B_DIM) for h in range(B_HEADS)]
    rr = lax.broadcasted_iota(jnp.int32, (GROUP_W, GROUP_W), 0) // B_DIM
    cc = lax.broadcasted_iota(jnp.int32, (GROUP_W, GROUP_W), 1) // B_DIM
    block_diag = rr == cc
    pos = lax.broadcasted_iota(jnp.int32, (c, 1), 0).astype(F32)
    ii = lax.broadcasted_iota(jnp.int32, (c, c), 0)
    jj = lax.broadcasted_iota(jnp.int32, (c, c), 1)

    c_decs = []
    for d in range(2):
        lg = -jnp.exp(dec_ref[d:d + 1, :])
        rel = (ii - jj) if d == 0 else (jj - ii)
        relf = jnp.maximum(rel, 0).astype(F32)
        for h in range(B_HEADS):
            intra_scr[d, h * c:(h + 1) * c, :] = jnp.where(
                rel >= 0, jnp.exp(relf * lg[:, h * B_DIM:h * B_DIM + 1]), 0.0)
        if d == 0:
            dec_scr[d, 0] = jnp.exp((pos + 1.0) * lg)
            dec_scr[d, 1] = jnp.exp((c - 1.0 - pos) * lg)
        else:
            dec_scr[d, 0] = jnp.exp((c - pos) * lg)
            dec_scr[d, 1] = jnp.exp(pos * lg)
        c_decs.append(jnp.exp(float(c) * lg))
    s_scr[...] = s0_ref[...]

    def chunk(i, carry):
        for bi in range(nb):
            for d in range(2):
                ci = i if d == 0 else nch - 1 - i
                r = pl.multiple_of(ci * c, c)
                qc = q_ref[bi, pl.ds(r, c), :]
                kc = k_ref[bi, pl.ds(r, c), :]
                vc = v_ref[bi, pl.ds(r, c), :]
                qb = qc.astype(BF16)
                q4 = jnp.concatenate([jnp.where(mk, qb, jnp.zeros_like(qb)) for mk in head_masks], axis=0)
                sc = (_dot_nt(q4, kc.astype(BF16)) * intra_scr[d]).astype(BF16)
                state = s_scr[bi, d]
                out = _dot((qc * dec_scr[d, 0]).astype(BF16), state.astype(BF16))
                for h in range(B_HEADS):
                    out = out + jnp.where(head_masks[h], _dot(sc[h * c:(h + 1) * c], vc), 0.0)
                o_scr[bi, d, pl.ds(r, c), :] = out
                kd_t = (kc * dec_scr[d, 1]).T.astype(BF16)
                s_scr[bi, d] = c_decs[d] * state + jnp.where(block_diag, _dot(kd_t, vc), 0.0)
        return carry

    lax.fori_loop(0, nch, chunk, 0)
    st_ref[...] = s_scr[...]

    def finish(i, carry):
        r = pl.multiple_of(i * c, c)
        for bi in range(nb):
            o = o_scr[bi, 0, pl.ds(r, c), :] + o_scr[bi, 1, pl.ds(r, c), :]
            y = _silu(gate_ref[bi, pl.ds(r, c), :]) * (o * lax.rsqrt(_head_mean(o * o, B_DIM) + EPS))
            y_ref[bi, pl.ds(r, c), :] = y.astype(BF16)
        return carry

    lax.fori_loop(0, nch, finish, 0)


def _retention(q, k, v, gate, dec_lanes, s0):
    b, n, _ = q.shape
    nb = RET_BATCHES if b % RET_BATCHES == 0 else 1
    seq = pl.BlockSpec((nb, n, GROUP_W), lambda bi: (bi, 0, 0))
    st = pl.BlockSpec((nb, 2, GROUP_W, GROUP_W), lambda bi: (bi, 0, 0, 0))
    return pl.pallas_call(
        _ret_kernel,
        grid=(b // nb,),
        in_specs=[seq, seq, seq, seq, pl.BlockSpec((2, GROUP_W), lambda bi: (0, 0)), st],
        out_specs=[seq, st],
        out_shape=[jax.ShapeDtypeStruct((b, n, GROUP_W), BF16),
                   jax.ShapeDtypeStruct((b, 2, GROUP_W, GROUP_W), F32)],
        scratch_shapes=[pltpu.VMEM((nb, 2, n, GROUP_W), F32), pltpu.VMEM((nb, 2, GROUP_W, GROUP_W), F32),
                        pltpu.VMEM((2, B_HEADS * RET_CHUNK, RET_CHUNK), F32),
                        pltpu.VMEM((2, 2, RET_CHUNK, GROUP_W), F32)],
        compiler_params=_cparams("arbitrary"),
        name="retention",
    )(q, k, v, gate, dec_lanes, s0)


def _convpool_kernel(uc_ref, ud_ref, cw_ref, cb_ref, lg_ref, lb_ref, pw_ref, ps_ref, yc_ref, yd_ref,
                     ubuf, dbuf, shift_scr):
    n = uc_ref.shape[1]
    rb = CONV_ROW_BLOCK
    for buf in (ubuf, dbuf):
        buf[0:PAD_ROWS, :] = jnp.zeros((PAD_ROWS, GROUP_W), F32)
        buf[PAD_ROWS + n:, :] = jnp.zeros((TAIL_ROWS, GROUP_W), F32)

    def fill(i, carry):
        r = pl.multiple_of(i * RET_CHUNK, RET_CHUNK)
        u2 = uc_ref[0, pl.ds(r, RET_CHUNK), :]
        ubuf[pl.ds(PAD_ROWS + r, RET_CHUNK), :] = u2[:, :GROUP_W] * _sigmoid(u2[:, GROUP_W:])
        dbuf[pl.ds(PAD_ROWS + r, RET_CHUNK), :] = ud_ref[0, pl.ds(r, RET_CHUNK), :]
        return carry

    lax.fori_loop(0, n // RET_CHUNK, fill, 0)

    lane = lax.broadcasted_iota(jnp.int32, (1, GROUP_W), 1)
    half = jnp.left_shift(1, lane // POOL_GROUP)
    first_tap = PAD_ROWS - C_KSIZE // 2

    def block(i, carry):
        r = pl.multiple_of(i * rb, rb)
        uwin = ubuf[pl.ds(r, rb + 2 * PAD_ROWS), :]
        dwin = dbuf[pl.ds(r, rb + PAD_ROWS + TAIL_ROWS), :]
        acc = jnp.zeros((rb, GROUP_W), F32) + cb_ref[...]
        for ph in range(SUBLANES):
            shift_scr[ph] = uwin[ph:ph + rb + 2 * PAD_ROWS - SUBLANES]
            for a8 in range(0, 2 * PAD_ROWS, SUBLANES):
                kk = a8 + ph - first_tap
                if 0 <= kk < C_KSIZE:
                    acc = acc + shift_scr[ph, a8:a8 + rb, :] * cw_ref[kk:kk + 1, :]
        mu = jnp.mean(acc, axis=-1, keepdims=True)
        var = jnp.mean(jnp.square(acc - mu), axis=-1, keepdims=True)
        yn = (acc - mu) * lax.rsqrt(var + EPS) * lg_ref[...] + lb_ref[...]
        yc_ref[0, pl.ds(r, rb), :] = _silu(yn).astype(BF16)
        d0 = dwin[:, :LANES]
        d1 = dwin[:, LANES:]
        p = PAD_ROWS
        s2 = d0[p - 1:p - 1 + rb] + d0[p:p + rb]
        s4 = s2 + d0[p - 2:p - 2 + rb] + d0[p + 1:p + 1 + rb]
        q2 = d1[0:rb + 32] + d1[1:rb + 33]
        q4 = q2[0:rb + 24] + q2[2:rb + 26]
        q8 = q4[0:rb + 16] + q4[4:rb + 20]
        s8 = q8[p - 4:p - 4 + rb]
        s16 = q8[p - 8:p - 8 + rb] + q8[p:p + rb]
        lane1 = lane[:, :LANES]
        tot = jnp.concatenate([jnp.where(lane1 < POOL_GROUP, s2, s4), jnp.where(lane1 < POOL_GROUP, s8, s16)],
                              axis=1)
        t = r + lax.broadcasted_iota(jnp.int32, (rb, 1), 0)
        cnt = (jnp.minimum(t + half, n) - jnp.maximum(t - half, 0)).astype(F32)
        pooled = tot / cnt - dwin[p:p + rb]
        yd_ref[0, pl.ds(r, rb), :] = (_dot(pooled.astype(BF16), pw_ref[...]) * ps_ref[...]).astype(BF16)
        return carry

    lax.fori_loop(0, n // rb, block, 0)


def _convpool(uc, ud, cw, cb, lng, lnb, pw_bd, pscale):
    b, n, _ = ud.shape
    seq = lambda w: pl.BlockSpec((1, n, w), lambda bi: (bi, 0, 0))
    row = pl.BlockSpec((1, GROUP_W), lambda bi: (0, 0))
    return pl.pallas_call(
        _convpool_kernel,
        grid=(b,),
        in_specs=[seq(2 * GROUP_W), seq(GROUP_W),
                  pl.BlockSpec((C_KSIZE, GROUP_W), lambda bi: (0, 0)), row, row, row,
                  pl.BlockSpec((GROUP_W, GROUP_W), lambda bi: (0, 0)), row],
        out_specs=[seq(GROUP_W), seq(GROUP_W)],
        out_shape=[jax.ShapeDtypeStruct((b, n, GROUP_W), BF16)] * 2,
        scratch_shapes=[pltpu.VMEM((n + PAD_ROWS + TAIL_ROWS, GROUP_W), F32)] * 2
                       + [pltpu.VMEM((SUBLANES, CONV_ROW_BLOCK + 2 * PAD_ROWS - SUBLANES, GROUP_W), F32)],
        compiler_params=_cparams("arbitrary"),
        name="conv_pool",
    )(uc, ud, cw, cb, lng, lnb, pw_bd, pscale)


def _tail_kernel(per_seq, final, *refs):
    y_refs = refs[0:12]
    x_ref, xp_ref, xn_ref = refs[12:15]
    wo_ref, wu_ref, dw_ref, db_ref, wd_ref, mod_ref, g_ref, fg_ref, o_ref = refs[15:24]
    ybuf, xbuf, hbuf, ubuf_a, ubuf_b = refs[24:]
    acc = o_ref
    tm = x_ref.shape[0]
    hr = BF16_ROWS
    i = pl.program_id(0)
    first = (i % per_seq) == 0
    last = (i % per_seq) == per_seq - 1

    for gi in range(N_MIXERS):
        y_ref, yp_ref, yn_ref = y_refs[3 * gi:3 * gi + 3]
        ybuf[gi, 0:hr, :] = yp_ref[...]
        ybuf[gi, hr:hr + tm, :] = y_ref[...]
        ybuf[gi, hr + tm:, :] = yn_ref[...]
    xbuf[0:hr, :] = xp_ref[...]
    xbuf[hr:hr + tm, :] = x_ref[...]
    xbuf[hr + tm:, :] = xn_ref[...]
    y = jnp.zeros(xbuf.shape, F32)
    for gi in range(N_MIXERS):
        y = y + _dot(ybuf[gi], wo_ref[0, gi * GROUP_W:(gi + 1) * GROUP_W, :])
    xbuf[...] = xbuf[...] + mod_ref[0, 2:3, :] * y
    h = _norm_mod(xbuf[...], g_ref[...], mod_ref[0, 3:4, :], mod_ref[0, 4:5, :]).astype(BF16)
    row = lax.broadcasted_iota(jnp.int32, (tm + 2 * hr, 1), 0)
    outside = (first & (row < hr)) | (last & (row >= hr + tm))
    hbuf[...] = jnp.where(outside, jnp.zeros_like(h), h)

    acc[...] = jnp.zeros_like(acc)
    tf = FFN_CHUNK
    nj = D_FF // tf

    def cols(j, half):
        return pl.ds(pl.multiple_of(half * D_FF + j * tf, LANES), tf)

    def up(j, buf):
        buf[:, 0:tf] = _dot(hbuf[...], wu_ref[0, :, cols(j, 0)])
        buf[:, tf:] = _dot(hbuf[...], wu_ref[0, :, cols(j, 1)])

    def mix(j, buf):
        w = jnp.concatenate([dw_ref[:, cols(j, 0)], dw_ref[:, cols(j, 1)]], axis=1)
        u = jnp.concatenate([db_ref[:, cols(j, 0)], db_ref[:, cols(j, 1)]], axis=1)
        for kk in range(FFN_KSIZE):
            u = u + buf[pl.ds(hr - FFN_KSIZE // 2 + kk, tm), :] * w[kk:kk + 1, :]
        act = (u[:, :tf] * _silu(u[:, tf:])).astype(BF16)
        acc[...] = acc[...] + _dot(act, wd_ref[0, pl.ds(pl.multiple_of(j * tf, tf), tf), :])

    up(0, ubuf_a)

    def pair(p, carry):
        up(2 * p + 1, ubuf_b)
        mix(2 * p, ubuf_a)
        up(2 * p + 2, ubuf_a)
        mix(2 * p + 1, ubuf_b)
        return carry

    pair(0, 0)
    lax.fori_loop(1, (nj - 1) // 2, pair, 0)
    mix(nj - 1, ubuf_a)
    x = xbuf[hr:hr + tm, :] + mod_ref[0, 5:6, :] * acc[...]
    if final:
        ms = jnp.mean(x * x, axis=-1, keepdims=True)
        x = x * lax.rsqrt(ms + EPS) * fg_ref[...]
    o_ref[...] = x


def _tail(ys, w_out_bf, wu_r, dw_r, db_r, wd_r, layer, x2, mod, g2, final_g, seq, final):
    t = x2.shape[0]
    tm = min(TOKEN_TILE, seq)
    per_seq = seq // tm
    nblk = tm // BF16_ROWS
    nj = D_FF // FFN_CHUNK
    assert nj % 2 == 1
    tok = lambda i: (i, 0)
    prev = lambda i: (jnp.maximum(i * nblk - 1, 0), 0)
    nxt = lambda i: (jnp.minimum((i + 1) * nblk, t // BF16_ROWS - 1), 0)
    const = lambda shape: pl.BlockSpec(shape, lambda i: (0,) * len(shape), pipeline_mode=pl.Buffered(1))
    stacked = lambda shape: pl.BlockSpec((1,) + shape, lambda i: (layer,) + (0,) * len(shape),
                                         pipeline_mode=pl.Buffered(1))
    trio = lambda w: [pl.BlockSpec((tm, w), tok), pl.BlockSpec((BF16_ROWS, w), prev), pl.BlockSpec((BF16_ROWS, w), nxt)]
    halo_rows = tm + 2 * BF16_ROWS
    args = []
    for y in ys:
        args += [y, y, y]
    return pl.pallas_call(
        functools.partial(_tail_kernel, per_seq, final),
        grid=(t // tm,),
        in_specs=trio(GROUP_W) * N_MIXERS + trio(D_MODEL)
                 + [stacked((D_MODEL, D_MODEL)),
                    stacked((D_MODEL, 2 * D_FF)), const((FFN_KSIZE, 2 * D_FF)),
                    const((1, 2 * D_FF)), stacked((D_FF, D_MODEL)),
                    pl.BlockSpec((1, N_MOD, D_MODEL), lambda i: (i // per_seq, 0, 0)),
                    const((1, D_MODEL)), const((1, D_MODEL))],
        out_specs=pl.BlockSpec((tm, D_MODEL), tok),
        out_shape=jax.ShapeDtypeStruct((t, D_MODEL), F32),
        scratch_shapes=[pltpu.VMEM((N_MIXERS, halo_rows, GROUP_W), BF16),
                        pltpu.VMEM((halo_rows, D_MODEL), F32),
                        pltpu.VMEM((halo_rows, D_MODEL), BF16),
                        pltpu.VMEM((halo_rows, 2 * FFN_CHUNK), F32),
                        pltpu.VMEM((halo_rows, 2 * FFN_CHUNK), F32)],
        compiler_params=_cparams("arbitrary"),
        name="out_ffn",
    )(*args, x2, x2, x2, w_out_bf, wu_r, dw_r, db_r, wd_r, mod, g2, final_g)


def kernel(x, c, ctx, c_ctx, w_mod, b_mod, norm1_g, norm2_g, w_in, w_out, diff_lambda, diff_subln_g, ret_decay,
           conv_dw_w, conv_dw_b, conv_ln_g, conv_ln_b, pool_w, pool_scale, ffn_w_up, ffn_dw_w, ffn_dw_b,
           ffn_w_down, final_g):
    b, n, d = x.shape
    nc = ctx.shape[1]
    depth = w_mod.shape[0]
    assert d == D_MODEL and b + 1 <= MOD_ROWS
    assert n % RET_CHUNK == 0 and nc % RET_CHUNK == 0 and n % min(TOKEN_TILE, n) == 0

    cond = jnp.zeros((MOD_ROWS, D_MODEL), F32).at[:b].set(c).at[b].set(c_ctx)
    mods = _modulation(cond, w_mod, b_mod)

    tab_x = _rope_tables(n, A_QKDIM // 4) + _rope_tables(n, B_DIM // 4)
    one, zero = jnp.ones((nc, LANES), F32), jnp.zeros((nc, LANES), F32)
    tab_c = (one, zero, one, zero)

    x2 = x.reshape(b * n, d)
    c2 = ctx.reshape(b * nc, d)
    row = lambda v: v.reshape(1, -1)
    g3 = lambda a, s: a.reshape(b, s, GROUP_W)
    zero_state = jnp.zeros((b, 2, GROUP_W, GROUP_W), F32)

    w_in_bf, w_out_bf = w_in.astype(BF16), w_out.astype(BF16)
    wu_bf, wd_bf = ffn_w_up.astype(BF16), ffn_w_down.astype(BF16)

    for l in range(depth):
        need_ctx = l < depth - 1
        lambda_init = 0.8 - 0.6 * math.exp(-0.3 * l)
        mod_x = mods[l, :b].reshape(b, N_MOD, D_MODEL)
        mod_c = jnp.broadcast_to(mods[l, b].reshape(1, N_MOD, D_MODEL), (b, N_MOD, D_MODEL))
        dw_r = ffn_dw_w[l]
        db_r = ffn_dw_b[l].reshape(1, -1)
        dec_lanes = jnp.repeat(ret_decay[l], B_DIM, axis=-1)
        pw_bd = jnp.zeros((GROUP_W, GROUP_W), F32)
        for gi in range(len(POOL_WINDOWS)):
            pw_bd = pw_bd.at[gi * POOL_GROUP:(gi + 1) * POOL_GROUP, gi * POOL_GROUP:(gi + 1) * POOL_GROUP].set(
                pool_w[l, gi])
        pw_bd = pw_bd.astype(BF16)
        mixer_params = (conv_dw_w[l], row(conv_dw_b[l]), row(conv_ln_g[l]), row(conv_ln_b[l]), pw_bd,
                        row(pool_scale[l]))

        pc = _in_proj(c2, mod_c, row(norm1_g[l]), w_in_bf, l, tab_c, nc, n + nc, n)
        aq_c, ak_part, av_part, bq_c, bk_c, bv_c, bg_c, uc_c, ud_c = pc
        px = _in_proj(x2, mod_x, row(norm1_g[l]), w_in_bf, l, tab_x, n, n + nc, 0, kv_bufs=(ak_part, av_part))
        aq_x, ak_all, av_all, bq_x, bk_x, bv_x, bg_x, uc_x, ud_x = px

        ya_x = _attention(g3(aq_x, n), ak_all, av_all, diff_lambda[l], diff_subln_g[l], lambda_init)
        yb_c, st_c = _retention(g3(bq_c, nc), g3(bk_c, nc), g3(bv_c, nc), g3(bg_c, nc), dec_lanes, zero_state)
        yb_x, _ = _retention(g3(bq_x, n), g3(bk_x, n), g3(bv_x, n), g3(bg_x, n), dec_lanes, st_c)
        yc_x, yd_x = _convpool(uc_x.reshape(b, n, 2 * GROUP_W), g3(ud_x, n), *mixer_params)
        flat = lambda a: a.reshape(-1, GROUP_W)
        x2 = _tail([flat(ya_x), flat(yb_x), flat(yc_x), flat(yd_x)], w_out_bf, wu_bf, dw_r, db_r, wd_bf, l, x2,
                   mod_x, row(norm2_g[l]), row(final_g), n, final=not need_ctx)

        if need_ctx:
            ya_c = _attention(g3(aq_c, nc), ak_all, av_all, diff_lambda[l], diff_subln_g[l], lambda_init,
                              key_rows=(n, nc))
            yc_c, yd_c = _convpool(uc_c.reshape(b, nc, 2 * GROUP_W), g3(ud_c, nc), *mixer_params)
            c2 = _tail([flat(ya_c), flat(yb_c), flat(yc_c), flat(yd_c)], w_out_bf, wu_bf, dw_r, db_r, wd_bf, l, c2,
                       mod_c, row(norm2_g[l]), row(final_g), nc, final=False)

    return x2.reshape(b, n, d)
```
